```python
import math
import jax, jax.numpy as jnp
from jax import lax
import numpy as np

D_MODEL = 1024
BATCH = 2
SEQ = 16384
DEPTH = 2
DEC_BATCH = 8
DEC_SEQ = 64
PAST_LEN = 2048

CHUNK = 64
D_MIX = D_MODEL
D_A = D_MIX // 2
N_HEADS_A = 4
DV_A = D_A // N_HEADS_A
D_HEAD_A = DV_A // 2
QK_W = N_HEADS_A * 2 * D_HEAD_A
D_B = D_MIX - D_A
N_GROUPS_B = 4
GROUP_W_B = D_B // N_GROUPS_B
CHUNK_MLP = 128
PROJ_COLS = 2 * QK_W + D_A + 2 * D_B
Q_BLOCK = 128
D_FF = 2816
N_EXPERTS = 8
TOP_K = 2
D_FF_E = 3584
N_DENSE = (DEPTH + 1) // 2
N_MOE = DEPTH // 2
ALPHA = (2.0 * DEPTH) ** 0.25
BETA = (8.0 * DEPTH) ** -0.25
LN_EPS = 1e-5
NEG_INF = -1e30

kernel_name = 'hybrid_diffattn_gmlp_stream_step'


def layer_norm(x, g, b):
    xf = x.astype(jnp.float32)
    mu = jnp.mean(xf, -1, keepdims=True)
    var = jnp.mean(jnp.square(xf - mu), -1, keepdims=True)
    return ((xf - mu) * lax.rsqrt(var + LN_EPS) * g.astype(jnp.float32) + b.astype(jnp.float32)).astype(x.dtype)


def rms_norm(x, g):
    xf = x.astype(jnp.float32)
    return (xf * lax.rsqrt(jnp.mean(jnp.square(xf), -1, keepdims=True) + LN_EPS) * g.astype(jnp.float32)).astype(x.dtype)


def lambda_init(layer):
    return 0.8 - 0.6 * math.exp(-0.3 * layer)


def diff_lambda(lq1, lk1, lq2, lk2, lam0):
    f = lambda a: a.astype(jnp.float32)
    return jnp.exp(jnp.sum(f(lq1) * f(lk1))) - jnp.exp(jnp.sum(f(lq2) * f(lk2))) + lam0


def project(h, w_in):
    B, S = h.shape[0], h.shape[1]
    z = jnp.einsum('bsd,dp->bsp', h, w_in)
    q = z[..., :QK_W].reshape(B, S, N_HEADS_A, 2, D_HEAD_A)
    k = z[..., QK_W:2 * QK_W].reshape(B, S, N_HEADS_A, 2, D_HEAD_A)
    v = z[..., 2 * QK_W:2 * QK_W + D_A].reshape(B, S, N_HEADS_A, DV_A)
    off = 2 * QK_W + D_A
    u = z[..., off:off + D_B]
    gv = z[..., off + D_B:]
    return q, k, v, u, gv


def diff_attn_block(qb, k, v, lam, mask):
    s = jnp.einsum('bqhtd,bkhtd->bhtqk', qb, k).astype(jnp.float32) * (D_HEAD_A ** -0.5)
    if mask is not None:
        s = jnp.where(mask[None, None, None], s, NEG_INF)
    p = jax.nn.softmax(s, axis=-1)
    a = p[:, :, 0] - lam * p[:, :, 1]
    return jnp.einsum('bhqk,bkhe->bqhe', a.astype(v.dtype), v)


def diff_attn_prompt(q, k, v, lam):
    B, S = q.shape[0], q.shape[1]
    nb = S // Q_BLOCK
    qb = q.reshape(B, nb, Q_BLOCK, N_HEADS_A, 2, D_HEAD_A).transpose(1, 0, 2, 3, 4, 5)
    k_chunk = jnp.arange(S) // CHUNK

    def one_block(args):
        blk, qblk = args
        q_chunk = (blk * Q_BLOCK + jnp.arange(Q_BLOCK)) // CHUNK
        mask = k_chunk[None, :] <= q_chunk[:, None]
        return diff_attn_block(qblk, k, v, lam, mask)

    o = lax.map(one_block, (jnp.arange(nb), qb))
    return o.transpose(1, 0, 2, 3, 4).reshape(B, S, N_HEADS_A, DV_A)


def spatial_gate(u, gv, ln_g, ln_b, w_s, b_s, n_pos):
    B, S = u.shape[0], u.shape[1]
    u = jax.nn.gelu(u)
    gv = layer_norm(jax.nn.gelu(gv), ln_g, ln_b)
    tril = jnp.tril(jnp.ones((n_pos, n_pos), w_s.dtype))
    wm = w_s[:, :n_pos, :n_pos] * tril
    gvc = gv.reshape(B, S // n_pos, n_pos, N_GROUPS_B, GROUP_W_B)
    mixed = jnp.einsum('gij,bnjgc->bnigc', wm, gvc) + b_s[:, :n_pos].T[None, None, :, :, None]
    return u * mixed.reshape(B, S, D_B), gv


def merge_heads(o, lam0, subln_g, gated, w_out):
    B, S = o.shape[0], o.shape[1]
    a = (rms_norm(o, subln_g) * (1.0 - lam0)).reshape(B, S, D_A)
    return jnp.einsum('bsm,md->bsd', jnp.concatenate([a, gated], axis=-1), w_out)


def swiglu(h, wg, wu, wd):
    return jnp.einsum('bsf,fd->bsd', jax.nn.silu(jnp.einsum('bsd,df->bsf', h, wg)) * jnp.einsum('bsd,df->bsf', h, wu), wd)


def moe_swiglu(h, w_router, wg, wu, wd):
    logits = jnp.einsum('bsd,de->bse', h, w_router).astype(jnp.float32)
    top_vals, top_idx = lax.top_k(logits, TOP_K)
    gates = jax.nn.softmax(top_vals, axis=-1)
    dense_gate = jnp.sum(jax.nn.one_hot(top_idx, N_EXPERTS, dtype=jnp.float32) * gates[..., None], axis=-2)
    out = jnp.zeros_like(h)
    for e in range(N_EXPERTS):
        out = out + dense_gate[..., e:e + 1].astype(h.dtype) * swiglu(h, wg[e], wu[e], wd[e])
    return out


def channel_mix(h, l, w_ffn_gate, w_ffn_up, w_ffn_down, w_router, w_moe_gate, w_moe_up, w_moe_down):
    i = l // 2
    if l % 2 == 0:
        return swiglu(h, w_ffn_gate[i], w_ffn_up[i], w_ffn_down[i])
    return moe_swiglu(h, w_router[i], w_moe_gate[i], w_moe_up[i], w_moe_down[i])


def setup_inputs(seed: int = 0) -> dict:
    key = jax.random.key(seed)
    ks = jax.random.split(key, 32)
    nrm = lambda k, shape, scale: jax.random.normal(k, shape, jnp.float32) * scale
    return {
        'x_prompt': nrm(ks[0], (BATCH, SEQ, D_MODEL), 1.0),
        'x_sample': nrm(ks[1], (DEC_BATCH, DEC_SEQ, D_MODEL), 1.0),
        'cache_k': nrm(ks[2], (DEPTH, DEC_BATCH, PAST_LEN, N_HEADS_A, 2, D_HEAD_A), 1.0),
        'cache_v': nrm(ks[3], (DEPTH, DEC_BATCH, PAST_LEN, N_HEADS_A, DV_A), 1.0),
        'w_in': nrm(ks[4], (DEPTH, D_MODEL, PROJ_COLS), D_MODEL ** -0.5),
        'lam_q1': nrm(ks[5], (DEPTH, D_HEAD_A), 0.1),
        'lam_k1': nrm(ks[6], (DEPTH, D_HEAD_A), 0.1),
        'lam_q2': nrm(ks[7], (DEPTH, D_HEAD_A), 0.1),
        'lam_k2': nrm(ks[8], (DEPTH, D_HEAD_A), 0.1),
        'subln_g': 1.0 + nrm(ks[9], (DEPTH, DV_A), 0.01),
        'sgu_ln_g': 1.0 + nrm(ks[10], (DEPTH, D_B), 0.01),
        'sgu_ln_b': nrm(ks[11], (DEPTH, D_B), 0.01),
        'w_spatial': nrm(ks[12], (DEPTH, N_GROUPS_B, CHUNK_MLP, CHUNK_MLP), CHUNK_MLP ** -0.5),
        'b_spatial': 1.0 + nrm(ks[13], (DEPTH, N_GROUPS_B, CHUNK_MLP), 0.01),
        'w_out': nrm(ks[14], (DEPTH, D_MIX, D_MODEL), BETA * D_MIX ** -0.5),
        'ln_mix_g': 1.0 + nrm(ks[15], (DEPTH, D_MODEL), 0.01),
        'ln_mix_b': nrm(ks[16], (DEPTH, D_MODEL), 0.01),
        'w_ffn_gate': nrm(ks[17], (N_DENSE, D_MODEL, D_FF), D_MODEL ** -0.5),
        'w_ffn_up': nrm(ks[18], (N_DENSE, D_MODEL, D_FF), D_MODEL ** -0.5),
        'w_ffn_down': nrm(ks[19], (N_DENSE, D_FF, D_MODEL), BETA * D_FF ** -0.5),
        'w_router': nrm(ks[20], (N_MOE, D_MODEL, N_EXPERTS), D_MODEL ** -0.5),
        'w_moe_gate': nrm(ks[21], (N_MOE, N_EXPERTS, D_MODEL, D_FF_E), D_MODEL ** -0.5),
        'w_moe_up': nrm(ks[22], (N_MOE, N_EXPERTS, D_MODEL, D_FF_E), D_MODEL ** -0.5),
        'w_moe_down': nrm(ks[23], (N_MOE, N_EXPERTS, D_FF_E, D_MODEL), BETA * D_FF_E ** -0.5),
        'ln_ffn_g': 1.0 + nrm(ks[24], (DEPTH, D_MODEL), 0.01),
        'ln_ffn_b': nrm(ks[25], (DEPTH, D_MODEL), 0.01),
    }


def reference(x_prompt, x_sample, cache_k, cache_v, w_in, lam_q1, lam_k1, lam_q2, lam_k2, subln_g,
              sgu_ln_g, sgu_ln_b, w_spatial, b_spatial, w_out, ln_mix_g, ln_mix_b,
              w_ffn_gate, w_ffn_up, w_ffn_down, w_router, w_moe_gate, w_moe_up, w_moe_down,
              ln_ffn_g, ln_ffn_b):
    xp, xs = x_prompt, x_sample
    kp_list, vp_list, ks_list, vs_list, gvs_list = [], [], [], [], []
    for l in range(DEPTH):
        lam0 = lambda_init(l)
        lam = diff_lambda(lam_q1[l], lam_k1[l], lam_q2[l], lam_k2[l], lam0)

        q, k, v, u, gv = project(xp, w_in[l])
        o = diff_attn_prompt(q, k, v, lam)
        gated, _ = spatial_gate(u, gv, sgu_ln_g[l], sgu_ln_b[l], w_spatial[l], b_spatial[l], CHUNK_MLP)
        mix = merge_heads(o, lam0, subln_g[l], gated, w_out[l])
        xp = layer_norm(ALPHA * xp + mix, ln_mix_g[l], ln_mix_b[l])
        ffn = channel_mix(xp, l, w_ffn_gate, w_ffn_up, w_ffn_down, w_router, w_moe_gate, w_moe_up, w_moe_down)
        xp = layer_norm(ALPHA * xp + ffn, ln_ffn_g[l], ln_ffn_b[l])
        kp_list.append(k)
        vp_list.append(v)

        qs, k_s, v_s, us, gvs = project(xs, w_in[l])
        k_all = jnp.concatenate([cache_k[l].astype(k_s.dtype), k_s], axis=1)
        v_all = jnp.concatenate([cache_v[l].astype(v_s.dtype), v_s], axis=1)
        o_s = diff_attn_block(qs, k_all, v_all, lam, None)
        gated_s, gvn_s = spatial_gate(us, gvs, sgu_ln_g[l], sgu_ln_b[l], w_spatial[l], b_spatial[l], xs.shape[1])
        mix_s = merge_heads(o_s, lam0, subln_g[l], gated_s, w_out[l])
        xs = layer_norm(ALPHA * xs + mix_s, ln_mix_g[l], ln_mix_b[l])
        ffn_s = channel_mix(xs, l, w_ffn_gate, w_ffn_up, w_ffn_down, w_router, w_moe_gate, w_moe_up, w_moe_down)
        xs = layer_norm(ALPHA * xs + ffn_s, ln_ffn_g[l], ln_ffn_b[l])
        ks_list.append(k_s)
        vs_list.append(v_s)
        gvs_list.append(gvn_s)

    k_prompt_new = jnp.stack(kp_list, axis=0)
    v_prompt_new = jnp.stack(vp_list, axis=0)
    k_sample_new = jnp.stack(ks_list, axis=0)
    v_sample_new = jnp.stack(vs_list, axis=0)
    gv_sample_new = jnp.stack(gvs_list, axis=0)
    return (xp, xs, k_prompt_new, v_prompt_new, k_sample_new, v_sample_new, gv_sample_new)
```

```python
import functools
import math

import jax
import jax.numpy as jnp
from jax import lax
from jax.experimental import pallas as pl
from jax.experimental.pallas import tpu as pltpu

F32 = jnp.float32
BF16 = jnp.bfloat16

N_HEADS = 4
HEAD_W = 128
D_HEAD = 64
N_GROUPS = 4
GROUP_W = 128
CHUNK = 64
CHUNK_MLP = 128
N_EXPERTS = 8
LN_EPS = 1e-5
NEG_INF = -1e30

LANES = 128
MXU_DIM = 256
VMEM_LIMIT_BYTES = 56 * 1024 * 1024


def _lambda_init(layer):
    return 0.8 - 0.6 * math.exp(-0.3 * layer)


def _layer_norm(x, g, b):
    mu = jnp.mean(x, axis=-1, keepdims=True)
    xc = x - mu
    var = jnp.mean(xc * xc, axis=-1, keepdims=True)
    return xc * lax.rsqrt(var + LN_EPS) * g + b


def _params(*semantics):
    return pltpu.CompilerParams(dimension_semantics=semantics, vmem_limit_bytes=VMEM_LIMIT_BYTES)


def _proj_kernel(x_ref, w_ref, g_ref, b_ref, q_ref, k_ref, v_ref, kb_ref, vb_ref, u_ref, gv_ref,
                 *, qk_w, d_a, d_b):
    x = x_ref[...].astype(BF16)

    def cols(start, width):
        return jnp.dot(x, w_ref[:, start:start + width], preferred_element_type=F32)

    q_ref[...] = (cols(0, qk_w) * (D_HEAD ** -0.5)).astype(BF16)
    k = cols(qk_w, qk_w)
    k_ref[...] = k
    kb_ref[...] = k.astype(BF16)
    v = cols(2 * qk_w, d_a)
    v_ref[...] = v
    vb_ref[...] = v.astype(BF16)
    off = 2 * qk_w + d_a
    u_ref[...] = jax.nn.gelu(cols(off, d_b))
    gv = jax.nn.gelu(cols(off + d_b, d_b))
    gv_ref[...] = _layer_norm(gv, g_ref[...], b_ref[...])


def _proj(x, w_in, ln_g, ln_b, tm):
    t, d = x.shape
    qk_w = N_HEADS * HEAD_W
    d_a = N_HEADS * HEAD_W
    d_b = N_GROUPS * GROUP_W
    assert w_in.shape == (d, 2 * qk_w + d_a + 2 * d_b) and t % tm == 0
    row = lambda w: pl.BlockSpec((tm, w), lambda i: (i, 0))
    const = lambda shape: pl.BlockSpec(shape, lambda i: (0, 0))
    return pl.pallas_call(
        functools.partial(_proj_kernel, qk_w=qk_w, d_a=d_a, d_b=d_b),
        grid=(t // tm,),
        in_specs=[row(d), const(w_in.shape), const((1, d_b)), const((1, d_b))],
        out_specs=[row(qk_w), row(qk_w), row(d_a), row(qk_w), row(d_a), row(d_b), row(d_b)],
        out_shape=[
            jax.ShapeDtypeStruct((t, qk_w), BF16),
            jax.ShapeDtypeStruct((t, qk_w), F32),
            jax.ShapeDtypeStruct((t, d_a), F32),
            jax.ShapeDtypeStruct((t, qk_w), BF16),
            jax.ShapeDtypeStruct((t, d_a), BF16),
            jax.ShapeDtypeStruct((t, d_b), F32),
            jax.ShapeDtypeStruct((t, d_b), F32),
        ],
        compiler_params=_params("parallel"),
        name="proj",
    )(x, w_in, ln_g, ln_b)


def _stack_queries(q):
    lane = lax.broadcasted_iota(jnp.int32, q.shape, 1)
    zero = jnp.zeros_like(q)
    return jnp.concatenate([jnp.where(lane < D_HEAD, q, zero), jnp.where(lane >= D_HEAD, q, zero)], axis=0)


def _scores(qs, kb):
    return lax.dot_general(qs, kb, (((1,), (1,)), ((), ())), preferred_element_type=F32)


def _diff_merge(o1, o2, lam, g, lam0):
    o = o1 - lam * o2
    ms = jnp.mean(o * o, axis=-1, keepdims=True)
    return o * lax.rsqrt(ms + LN_EPS) * g * (1.0 - lam0)


def _attn_prompt_kernel(lam_ref, g_ref, q_ref, k_ref, v_ref, o_ref, acc_ref, m_ref, *, tq, lam0):
    i = pl.program_id(2)
    qs = _stack_queries(q_ref[...])
    m_ref[...] = jnp.full(m_ref.shape, NEG_INF, F32)
    acc_ref[...] = jnp.zeros(acc_ref.shape, F32)
    ones = jnp.ones((tq, HEAD_W), BF16)

    def block(j, masked):
        off = pl.multiple_of(j * tq, tq)
        kb = k_ref[pl.ds(off, tq), :]
        vb = v_ref[pl.ds(off, tq), :]
        s = _scores(qs, kb)
        if masked:
            q_chunk = (lax.broadcasted_iota(jnp.int32, s.shape, 0) % tq) // CHUNK
            k_chunk = lax.broadcasted_iota(jnp.int32, s.shape, 1) // CHUNK
            s = jnp.where(k_chunk <= q_chunk, s, NEG_INF)
        m_prev = m_ref[...]
        m_new = jnp.maximum(m_prev, jnp.max(s, axis=1, keepdims=True))
        p = jnp.exp(s - m_new).astype(BF16)
        pv = jnp.dot(p, jnp.concatenate([vb, ones], axis=1), preferred_element_type=F32)
        acc_ref[...] = jnp.exp(m_prev - m_new) * acc_ref[...] + pv
        m_ref[...] = m_new

    def body(j, carry):
        block(j, False)
        return carry

    lax.fori_loop(0, i, body, 0)
    block(i, True)

    acc = acc_ref[...]
    o1 = acc[:tq, :HEAD_W] / acc[:tq, HEAD_W:]
    o2 = acc[tq:, :HEAD_W] / acc[tq:, HEAD_W:]
    o_ref[...] = _diff_merge(o1, o2, lam_ref[0], g_ref[...], lam0).astype(o_ref.dtype)


def _attn_prompt(q, k, v, lam, subln_g, lam0, tq):
    b, s, w = q.shape
    assert s % tq == 0 and tq % CHUNK == 0 and w == N_HEADS * HEAD_W
    q_spec = pl.BlockSpec((None, tq, HEAD_W), lambda bi, h, i: (bi, i, h))
    kv_spec = pl.BlockSpec((None, s, HEAD_W), lambda bi, h, i: (bi, 0, h))
    return pl.pallas_call(
        functools.partial(_attn_prompt_kernel, tq=tq, lam0=lam0),
        grid=(b, N_HEADS, s // tq),
        in_specs=[
            pl.BlockSpec(memory_space=pltpu.SMEM),
            pl.BlockSpec((1, HEAD_W), lambda bi, h, i: (0, 0)),
            q_spec, kv_spec, kv_spec,
        ],
        out_specs=q_spec,
        out_shape=jax.ShapeDtypeStruct((b, s, w), BF16),
        scratch_shapes=[pltpu.VMEM((2 * tq, 2 * HEAD_W), F32), pltpu.VMEM((2 * tq, 1), F32)],
        compiler_params=_params("parallel", "parallel", "arbitrary"),
        name="attn_prompt",
    )(lam, subln_g, q, k, v)


def _attn_sample_kernel(lam_ref, g_ref, q_ref, k_ref, v_ref, o_ref, *, n_q, n_keys, lam0):
    qs = _stack_queries(q_ref[...])
    s = _scores(qs, k_ref[...])
    col = lax.broadcasted_iota(jnp.int32, s.shape, 1)
    s = jnp.where(col < n_keys, s, NEG_INF)
    m = jnp.max(s, axis=1, keepdims=True)
    p = jnp.exp(s - m)
    l = jnp.sum(p, axis=1, keepdims=True)
    o = jnp.dot(p.astype(BF16), v_ref[...], preferred_element_type=F32) / l
    o_ref[...] = _diff_merge(o[:n_q], o[n_q:], lam_ref[0], g_ref[...], lam0).astype(o_ref.dtype)


def _attn_sample(q, k_all, v_all, n_keys, lam, subln_g, lam0):
    b, n_q, w = q.shape
    n_pad = k_all.shape[1]
    q_spec = pl.BlockSpec((None, n_q, HEAD_W), lambda bi, h: (bi, 0, h))
    kv_spec = pl.BlockSpec((None, n_pad, HEAD_W), lambda bi, h: (bi, 0, h))
    return pl.pallas_call(
        functools.partial(_attn_sample_kernel, n_q=n_q, n_keys=n_keys, lam0=lam0),
        grid=(b, N_HEADS),
        in_specs=[
            pl.BlockSpec(memory_space=pltpu.SMEM),
            pl.BlockSpec((1, HEAD_W), lambda bi, h: (0, 0)),
            q_spec, kv_spec, kv_spec,
        ],
        out_specs=q_spec,
        out_shape=jax.ShapeDtypeStruct((b, n_q, w), BF16),
        compiler_params=_params("parallel", "parallel"),
        name="attn_sample",
    )(lam, subln_g, q, k_all, v_all)


def _merge_kernel(a_ref, u_ref, gv_ref, x_ref, wm_ref, bs_ref, wo_ref, g_ref, b_ref, y_ref, gated_ref,
                  *, tm, alpha):
    d_a = a_ref.shape[1]
    for c in range(tm // CHUNK_MLP):
        rows = slice(c * CHUNK_MLP, (c + 1) * CHUNK_MLP)
        for grp in range(N_GROUPS):
            cs = slice(grp * GROUP_W, (grp + 1) * GROUP_W)
            mixed = jnp.dot(wm_ref[grp], gv_ref[rows, cs].astype(BF16), preferred_element_type=F32)
            gated_ref[rows, cs] = (u_ref[rows, cs] * (mixed + bs_ref[:, cs])).astype(BF16)
    mix = jnp.dot(a_ref[...], wo_ref[:d_a, :], preferred_element_type=F32)
    mix = mix + jnp.dot(gated_ref[...], wo_ref[d_a:, :], preferred_element_type=F32)
    y_ref[...] = _layer_norm(alpha * x_ref[...] + mix, g_ref[...], b_ref[...])


def _merge(a, u, gvn, x, wm, bs, w_out, ln_g, ln_b, alpha, tm):
    t, d = x.shape
    d_a, d_b = a.shape[1], u.shape[1]
    assert t % tm == 0 and tm % CHUNK_MLP == 0
    row = lambda w: pl.BlockSpec((tm, w), lambda i: (i, 0))
    const = lambda shape: pl.BlockSpec(shape, lambda i: (0,) * len(shape))
    return pl.pallas_call(
        functools.partial(_merge_kernel, tm=tm, alpha=alpha),
        grid=(t // tm,),
        in_specs=[row(d_a), row(d_b), row(d_b), row(d), const(wm.shape), const(bs.shape),
                  const(w_out.shape), const((1, d)), const((1, d))],
        out_specs=row(d),
        out_shape=jax.ShapeDtypeStruct((t, d), F32),
        scratch_shapes=[pltpu.VMEM((tm, d_b), BF16)],
        compiler_params=_params("parallel"),
        name="merge",
    )(a, u, gvn, x, wm, bs, w_out, ln_g, ln_b)


def _ff_chunks(d_ff, width):
    return [(s, min(width, d_ff - s)) for s in range(0, d_ff, width)]


def _swiglu_tile(x, wg_ref, wu_ref, wd_ref, ff_chunk):
    acc = None
    for start, size in _ff_chunks(wg_ref.shape[1], ff_chunk):
        hg = jnp.dot(x, wg_ref[:, start:start + size], preferred_element_type=F32)
        hu = jnp.dot(x, wu_ref[:, start:start + size], preferred_element_type=F32)
        h = (jax.nn.silu(hg) * hu).astype(BF16)
        part = jnp.dot(h, wd_ref[start:start + size, :], preferred_element_type=F32)
        acc = part if acc is None else acc + part
    return acc


def _ffn_dense_kernel(x_ref, wg_ref, wu_ref, wd_ref, g_ref, b_ref, y_ref, *, alpha, ff_chunk):
    x = x_ref[...]
    ffn = _swiglu_tile(x.astype(BF16), wg_ref, wu_ref, wd_ref, ff_chunk)
    y_ref[...] = _layer_norm(alpha * x + ffn, g_ref[...], b_ref[...])


def _ffn_dense(x, wg, wu, wd, ln_g, ln_b, alpha, tm, ff_chunk):
    t, d = x.shape
    assert t % tm == 0
    row = pl.BlockSpec((tm, d), lambda i: (i, 0))
    const = lambda shape: pl.BlockSpec(shape, lambda i: (0, 0), pipeline_mode=pl.Buffered(1))
    return pl.pallas_call(
        functools.partial(_ffn_dense_kernel, alpha=alpha, ff_chunk=ff_chunk),
        grid=(t // tm,),
        in_specs=[row, const(wg.shape), const(wu.shape), const(wd.shape), const((1, d)), const((1, d))],
        out_specs=row,
        out_shape=jax.ShapeDtypeStruct((t, d), F32),
        compiler_params=_params("parallel"),
        name="ffn_dense",
    )(x, wg, wu, wd, ln_g, ln_b)


def _ffn_grouped_kernel(te_ref, nt_ref, x_ref, wg_ref, wu_ref, wd_ref, y_ref, *, ff_chunk):
    @pl.when(pl.program_id(0) < nt_ref[0])
    def _():
        y_ref[...] = _swiglu_tile(x_ref[...], wg_ref, wu_ref, wd_ref, ff_chunk)


def _ffn_grouped(xs, tile_expert, n_tiles_used, wg, wu, wd, tm, ff_chunk):
    r, d = xs.shape
    assert r % tm == 0
    row = pl.BlockSpec((tm, d), lambda i, te, nt: (i, 0))
    expert = lambda shape: pl.BlockSpec((None,) + shape[1:], lambda i, te, nt: (te[i], 0, 0),
                                        pipeline_mode=pl.Buffered(1))
    return pl.pallas_call(
        functools.partial(_ffn_grouped_kernel, ff_chunk=ff_chunk),
        grid_spec=pltpu.PrefetchScalarGridSpec(
            num_scalar_prefetch=2,
            grid=(r // tm,),
            in_specs=[row, expert(wg.shape), expert(wu.shape), expert(wd.shape)],
            out_specs=row,
        ),
        out_shape=jax.ShapeDtypeStruct((r, d), F32),
        compiler_params=_params("arbitrary"),
        name="ffn_grouped",
    )(tile_expert, n_tiles_used, xs, wg, wu, wd)


def _router_kernel(x_ref, wr_ref, e_ref, g_ref):
    logits = lax.dot_general(wr_ref[...], x_ref[...], (((1,), (1,)), ((), ())),
                             precision=lax.Precision.HIGHEST, preferred_element_type=F32)
    idx = lax.broadcasted_iota(jnp.int32, logits.shape, 0)
    m1 = jnp.max(logits, axis=0, keepdims=True)
    i1 = jnp.min(jnp.where(logits == m1, idx, N_EXPERTS), axis=0, keepdims=True)
    rest = jnp.where(idx == i1, -jnp.inf, logits)
    m2 = jnp.max(rest, axis=0, keepdims=True)
    i2 = jnp.min(jnp.where(rest == m2, idx, N_EXPERTS), axis=0, keepdims=True)
    e2 = jnp.exp(m2 - m1)
    denom = 1.0 + e2
    e_ref[...] = jnp.concatenate([i1, i2], axis=0)
    g_ref[...] = jnp.concatenate([1.0 / denom, e2 / denom], axis=0)


def _router(x, w_router_t, tm):
    t, d = x.shape
    assert t % tm == 0
    return pl.pallas_call(
        _router_kernel,
        grid=(t // tm,),
        in_specs=[pl.BlockSpec((tm, d), lambda i: (i, 0)), pl.BlockSpec(w_router_t.shape, lambda i: (0, 0))],
        out_specs=[pl.BlockSpec((2, tm), lambda i: (0, i)), pl.BlockSpec((2, tm), lambda i: (0, i))],
        out_shape=[jax.ShapeDtypeStruct((2, t), jnp.int32), jax.ShapeDtypeStruct((2, t), F32)],
        compiler_params=_params("parallel"),
        name="router",
    )(x, w_router_t)


def _combine_kernel(x_ref, y1_ref, y2_ref, g1_ref, g2_ref, lg_ref, lb_ref, o_ref, *, alpha):
    ffn = g1_ref[...] * y1_ref[...] + g2_ref[...] * y2_ref[...]
    o_ref[...] = _layer_norm(alpha * x_ref[...] + ffn, lg_ref[...], lb_ref[...])


def _combine(x, y1, y2, g1, g2, ln_g, ln_b, alpha, tm):
    t, d = x.shape
    row = pl.BlockSpec((tm, d), lambda i: (i, 0))
    col = pl.BlockSpec((tm, 1), lambda i: (i, 0))
    const = pl.BlockSpec((1, d), lambda i: (0, 0))
    return pl.pallas_call(
        functools.partial(_combine_kernel, alpha=alpha),
        grid=(t // tm,),
        in_specs=[row, row, row, col, col, const, const],
        out_specs=row,
        out_shape=jax.ShapeDtypeStruct((t, d), F32),
        compiler_params=_params("parallel"),
        name="combine",
    )(x, y1, y2, g1, g2, ln_g, ln_b)


def _moe(x, w_router_t, wg, wu, wd, ln_g, ln_b, alpha, tm_route, tm_ffn, ff_chunk):
    t, d = x.shape
    experts, gates = _router(x, w_router_t, tm_route)
    pair_e = experts.reshape(2 * t)
    onehot = (pair_e[:, None] == jnp.arange(N_EXPERTS, dtype=jnp.int32)[None, :]).astype(jnp.int32)
    csum = jnp.cumsum(onehot, axis=0)
    rank = jnp.take_along_axis(csum, pair_e[:, None], axis=1)[:, 0] - 1
    counts = csum[-1]
    tiles_per = (counts + tm_ffn - 1) // tm_ffn
    tile_end = jnp.cumsum(tiles_per)
    starts = (tile_end - tiles_per) * tm_ffn
    pos = starts[pair_e] + rank
    n_tiles = (2 * t) // tm_ffn + N_EXPERTS
    src = jnp.zeros((n_tiles * tm_ffn,), jnp.int32).at[pos].set(jnp.tile(jnp.arange(t, dtype=jnp.int32), 2))
    tile_expert = jnp.minimum(
        jnp.searchsorted(tile_end, jnp.arange(n_tiles, dtype=jnp.int32), side="right"), N_EXPERTS - 1
    ).astype(jnp.int32)
    xs = jnp.take(x.astype(BF16), src, axis=0)
    ys = _ffn_grouped(xs, tile_expert, tile_end[-1:].astype(jnp.int32), wg, wu, wd, tm_ffn, ff_chunk)
    y1 = jnp.take(ys, pos[:t], axis=0)
    y2 = jnp.take(ys, pos[t:], axis=0)
    return _combine(x, y1, y2, gates[0][:, None], gates[1][:, None], ln_g, ln_b, alpha, tm_route)


def _spatial_params(w_s, b_s, n_pos):
    reps = CHUNK_MLP // n_pos
    wm = w_s[:, :n_pos, :n_pos] * jnp.tril(jnp.ones((n_pos, n_pos), w_s.dtype))
    eye = jnp.eye(reps, dtype=w_s.dtype)
    wm = jnp.einsum("ab,gij->gaibj", eye, wm).reshape(N_GROUPS, CHUNK_MLP, CHUNK_MLP)
    bias = jnp.tile(b_s[:, :n_pos].T, (reps, 1))
    return wm.astype(BF16), jnp.repeat(bias, GROUP_W, axis=1).astype(F32)


def kernel(x_prompt, x_sample, cache_k, cache_v, w_in, lam_q1, lam_k1, lam_q2, lam_k2, subln_g,
           sgu_ln_g, sgu_ln_b, w_spatial, b_spatial, w_out, ln_mix_g, ln_mix_b,
           w_ffn_gate, w_ffn_up, w_ffn_down, w_router, w_moe_gate, w_moe_up, w_moe_down,
           ln_ffn_g, ln_ffn_b):
    depth = w_in.shape[0]
    batch, seq, d_model = x_prompt.shape
    dec_batch, dec_seq, _ = x_sample.shape
    past_len = cache_k.shape[2]
    alpha = (2.0 * depth) ** 0.25
    t_p, t_s = batch * seq, dec_batch * dec_seq
    n_keys = past_len + dec_seq
    keys_pad = -(-n_keys // LANES) * LANES

    tm_p = min(512, t_p)
    tm_s = min(512, t_s)
    tq = min(512, seq)

    xp = x_prompt.reshape(t_p, d_model)
    xs = x_sample.reshape(t_s, d_model)
    outs = {name: [] for name in ("kp", "vp", "ks", "vs", "gvs")}
    row2 = lambda a: a.reshape(1, -1).astype(F32)

    for l in range(depth):
        lam0 = _lambda_init(l)
        lam = (jnp.exp(jnp.sum(lam_q1[l].astype(F32) * lam_k1[l].astype(F32)))
               - jnp.exp(jnp.sum(lam_q2[l].astype(F32) * lam_k2[l].astype(F32))) + lam0).reshape(1)
        w_in_l = w_in[l].astype(BF16)
        w_out_l = w_out[l].astype(BF16)
        sub_g = row2(subln_g[l])
        sgu_g, sgu_b = row2(sgu_ln_g[l]), row2(sgu_ln_b[l])
        mix_g, mix_b = row2(ln_mix_g[l]), row2(ln_mix_b[l])
        ffn_g, ffn_b = row2(ln_ffn_g[l]), row2(ln_ffn_b[l])
        wm_p, bs_p = _spatial_params(w_spatial[l], b_spatial[l], CHUNK_MLP)
        wm_s, bs_s = _spatial_params(w_spatial[l], b_spatial[l], dec_seq)

        def channel_mix(x, tm):
            i = l // 2
            if l % 2 == 0:
                return _ffn_dense(x, w_ffn_gate[i].astype(BF16), w_ffn_up[i].astype(BF16),
                                  w_ffn_down[i].astype(BF16), ffn_g, ffn_b, alpha, tm, 512)
            return _moe(x, w_router[i].T.astype(F32), w_moe_gate[i].astype(BF16), w_moe_up[i].astype(BF16),
                        w_moe_down[i].astype(BF16), ffn_g, ffn_b, alpha, tm, min(512, 2 * x.shape[0]), 512)

        q, k, v, kb, vb, u, gvn = _proj(xp, w_in_l, sgu_g, sgu_b, tm_p)
        shape3 = (batch, seq, -1)
        a = _attn_prompt(q.reshape(shape3), kb.reshape(shape3), vb.reshape(shape3), lam, sub_g, lam0, tq)
        xp = _merge(a.reshape(t_p, -1), u, gvn, xp, wm_p, bs_p, w_out_l, mix_g, mix_b, alpha, tm_p)
        xp = channel_mix(xp, tm_p)
        outs["kp"].append(k.reshape(batch, seq, N_HEADS, 2, D_HEAD))
        outs["vp"].append(v.reshape(batch, seq, N_HEADS, HEAD_W))

        q, k, v, kb, vb, u, gvn = _proj(xs, w_in_l, sgu_g, sgu_b, tm_s)
        pad = lambda c, new: jnp.pad(
            jnp.concatenate([c.reshape(dec_batch, past_len, -1).astype(BF16),
                             new.reshape(dec_batch, dec_seq, -1)], axis=1),
            ((0, 0), (0, keys_pad - n_keys), (0, 0)))
        a = _attn_sample(q.reshape(dec_batch, dec_seq, -1), pad(cache_k[l], kb), pad(cache_v[l], vb),
                         n_keys, lam, sub_g, lam0)
        xs = _merge(a.reshape(t_s, -1), u, gvn, xs, wm_s, bs_s, w_out_l, mix_g, mix_b, alpha, tm_s)
        xs = channel_mix(xs, tm_s)
        outs["ks"].append(k.reshape(dec_batch, dec_seq, N_HEADS, 2, D_HEAD))
        outs["vs"].append(v.reshape(dec_batch, dec_seq, N_HEADS, HEAD_W))
        outs["gvs"].append(gvn.reshape(dec_batch, dec_seq, -1))

    stack = lambda name: jnp.stack(outs[name], axis=0)
    return (xp.reshape(batch, seq, d_model), xs.reshape(dec_batch, dec_seq, d_model),
            stack("kp"), stack("vp"), stack("ks"), stack("vs"), stack("gvs"))
```

```python
import functools
import math

import jax
import jax.numpy as jnp
from jax import lax
from jax.experimental import pallas as pl
from jax.experimental.pallas import tpu as pltpu

F32 = jnp.float32
BF16 = jnp.bfloat16

N_HEADS = 4
HEAD_W = 128
D_HEAD = 64
N_GROUPS = 4
GROUP_W = 128
CHUNK = 64
CHUNK_MLP = 128
N_EXPERTS = 8
LN_EPS = 1e-5
NEG_INF = -1e30

LANES = 128
MXU_DIM = 256
VMEM_LIMIT_BYTES = 56 * 1024 * 1024


def _lambda_init(layer):
    return 0.8 - 0.6 * math.exp(-0.3 * layer)


def _layer_norm(x, g, b):
    mu = jnp.mean(x, axis=-1, keepdims=True)
    xc = x - mu
    var = jnp.mean(xc * xc, axis=-1, keepdims=True)
    return xc * lax.rsqrt(var + LN_EPS) * g + b


def _params(*semantics):
    return pltpu.CompilerParams(dimension_semantics=semantics, vmem_limit_bytes=VMEM_LIMIT_BYTES)


def _proj_kernel(x_ref, w_ref, g_ref, b_ref, q_ref, k_ref, v_ref, kb_ref, vb_ref, u_ref, gv_ref,
                 *, qk_w, d_a, d_b):
    x = x_ref[...].astype(BF16)

    def cols(start, width):
        return jnp.dot(x, w_ref[:, start:start + width], preferred_element_type=F32)

    q_ref[...] = (cols(0, qk_w) * (D_HEAD ** -0.5)).astype(BF16)
    k = cols(qk_w, qk_w)
    k_ref[...] = k
    kb_ref[...] = k.astype(BF16)
    v = cols(2 * qk_w, d_a)
    v_ref[...] = v
    vb_ref[...] = v.astype(BF16)
    off = 2 * qk_w + d_a
    u_ref[...] = jax.nn.gelu(cols(off, d_b))
    gv = jax.nn.gelu(cols(off + d_b, d_b))
    gv_ref[...] = _layer_norm(gv, g_ref[...], b_ref[...])


def _proj(x, w_in, ln_g, ln_b, tm):
    t, d = x.shape
    qk_w = N_HEADS * HEAD_W
    d_a = N_HEADS * HEAD_W
    d_b = N_GROUPS * GROUP_W
    assert w_in.shape == (d, 2 * qk_w + d_a + 2 * d_b) and t % tm == 0
    row = lambda w: pl.BlockSpec((tm, w), lambda i: (i, 0))
    const = lambda shape: pl.BlockSpec(shape, lambda i: (0, 0))
    return pl.pallas_call(
        functools.partial(_proj_kernel, qk_w=qk_w, d_a=d_a, d_b=d_b),
        grid=(t // tm,),
        in_specs=[row(d), const(w_in.shape), const((1, d_b)), const((1, d_b))],
        out_specs=[row(qk_w), row(qk_w), row(d_a), row(qk_w), row(d_a), row(d_b), row(d_b)],
        out_shape=[
            jax.ShapeDtypeStruct((t, qk_w), BF16),
            jax.ShapeDtypeStruct((t, qk_w), F32),
            jax.ShapeDtypeStruct((t, d_a), F32),
            jax.ShapeDtypeStruct((t, qk_w), BF16),
            jax.ShapeDtypeStruct((t, d_a), BF16),
            jax.ShapeDtypeStruct((t, d_b), F32),
            jax.ShapeDtypeStruct((t, d_b), F32),
        ],
        compiler_params=_params("parallel"),
        name="proj",
    )(x, w_in, ln_g, ln_b)


def _stack_queries(q):
    lane = lax.broadcasted_iota(jnp.int32, q.shape, 1)
    zero = jnp.zeros_like(q)
    return jnp.concatenate([jnp.where(lane < D_HEAD, q, zero), jnp.where(lane >= D_HEAD, q, zero)], axis=0)


def _scores(qs, kb):
    return lax.dot_general(qs, kb, (((1,), (1,)), ((), ())), preferred_element_type=F32)


def _diff_merge(o1, o2, lam, g, lam0):
    o = o1 - lam * o2
    ms = jnp.mean(o * o, axis=-1, keepdims=True)
    return o * lax.rsqrt(ms + LN_EPS) * g * (1.0 - lam0)


def _attn_prompt_kernel(lam_ref, g_ref, q_ref, k_ref, v_ref, o_ref,
                        qs_ref, s_ref, p_ref, a_ref, m_ref, acc_ref, *, tq, lam0):
    i = pl.program_id(2)
    n_rep = tq // LANES
    qs_ref[...] = _stack_queries(q_ref[...])
    m_ref[...] = jnp.full(m_ref.shape, NEG_INF, F32)
    acc_ref[...] = jnp.zeros(acc_ref.shape, F32)
    p_ref[1] = jnp.zeros(p_ref.shape[1:], BF16)
    a_ref[1] = jnp.ones(a_ref.shape[1:], F32)
    ones = jnp.ones((tq, HEAD_W), BF16)

    def scores(j):
        off = pl.multiple_of(j * tq, tq)
        s_ref[...] = _scores(qs_ref[...], k_ref[pl.ds(off, tq), :])

    def softmax(masked, slot):
        s = s_ref[...]
        if masked:
            q_chunk = (lax.broadcasted_iota(jnp.int32, s.shape, 0) % tq) // CHUNK
            k_chunk = lax.broadcasted_iota(jnp.int32, s.shape, 1) // CHUNK
            s = jnp.where(k_chunk <= q_chunk, s, NEG_INF)
        m_prev = m_ref[...]
        m_new = jnp.maximum(m_prev, jnp.max(s, axis=1, keepdims=True))
        a_ref[slot] = jnp.exp(m_prev - m_new)
        p_ref[slot] = jnp.exp(s - jnp.concatenate([m_new] * n_rep, axis=1)).astype(BF16)
        m_ref[...] = m_new

    def pv(j, slot):
        off = pl.multiple_of(j * tq, tq)
        v_ext = jnp.concatenate([v_ref[pl.ds(off, tq), :], ones], axis=1)
        a = a_ref[slot]
        acc_ref[...] = (jnp.concatenate([a, a], axis=1) * acc_ref[...]
                        + jnp.dot(p_ref[slot], v_ext, preferred_element_type=F32))

    scores(0)

    def body(j, carry):
        slot = j % 2
        pv(jnp.maximum(j - 1, 0), 1 - slot)
        softmax(False, slot)
        scores(j + 1)
        return carry

    lax.fori_loop(0, i, body, 0)
    slot = i % 2
    pv(jnp.maximum(i - 1, 0), 1 - slot)
    softmax(True, slot)
    pv(i, slot)

    acc = acc_ref[...]
    o1 = acc[:tq, :HEAD_W] / acc[:tq, HEAD_W:]
    o2 = acc[tq:, :HEAD_W] / acc[tq:, HEAD_W:]
    o_ref[...] = _diff_merge(o1, o2, lam_ref[0], g_ref[...], lam0).astype(o_ref.dtype)


def _attn_prompt(q, k, v, lam, subln_g, lam0, tq):
    b, s, w = q.shape
    assert s % tq == 0 and tq % CHUNK == 0 and w == N_HEADS * HEAD_W
    q_spec = pl.BlockSpec((None, tq, HEAD_W), lambda bi, h, i: (bi, i, h))
    kv_spec = pl.BlockSpec((None, s, HEAD_W), lambda bi, h, i: (bi, 0, h))
    return pl.pallas_call(
        functools.partial(_attn_prompt_kernel, tq=tq, lam0=lam0),
        grid=(b, N_HEADS, s // tq),
        in_specs=[
            pl.BlockSpec(memory_space=pltpu.SMEM),
            pl.BlockSpec((1, HEAD_W), lambda bi, h, i: (0, 0)),
            q_spec, kv_spec, kv_spec,
        ],
        out_specs=q_spec,
        out_shape=jax.ShapeDtypeStruct((b, s, w), BF16),
        scratch_shapes=[
            pltpu.VMEM((2 * tq, HEAD_W), BF16),
            pltpu.VMEM((2 * tq, tq), F32),
            pltpu.VMEM((2, 2 * tq, tq), BF16),
            pltpu.VMEM((2, 2 * tq, LANES), F32),
            pltpu.VMEM((2 * tq, LANES), F32),
            pltpu.VMEM((2 * tq, 2 * HEAD_W), F32),
        ],
        compiler_params=_params("parallel", "parallel", "arbitrary"),
        name="attn_prompt",
    )(lam, subln_g, q, k, v)


def _attn_sample_kernel(lam_ref, g_ref, q_ref, k_ref, v_ref, o_ref, *, n_q, n_keys, lam0):
    qs = _stack_queries(q_ref[...])
    s = _scores(qs, k_ref[...])
    col = lax.broadcasted_iota(jnp.int32, s.shape, 1)
    s = jnp.where(col < n_keys, s, NEG_INF)
    m = jnp.max(s, axis=1, keepdims=True)
    p = jnp.exp(s - m)
    l = jnp.sum(p, axis=1, keepdims=True)
    o = jnp.dot(p.astype(BF16), v_ref[...], preferred_element_type=F32) / l
    o_ref[...] = _diff_merge(o[:n_q], o[n_q:], lam_ref[0], g_ref[...], lam0).astype(o_ref.dtype)


def _attn_sample(q, k_all, v_all, n_keys, lam, subln_g, lam0):
    b, n_q, w = q.shape
    n_pad = k_all.shape[1]
    q_spec = pl.BlockSpec((None, n_q, HEAD_W), lambda bi, h: (bi, 0, h))
    kv_spec = pl.BlockSpec((None, n_pad, HEAD_W), lambda bi, h: (bi, 0, h))
    return pl.pallas_call(
        functools.partial(_attn_sample_kernel, n_q=n_q, n_keys=n_keys, lam0=lam0),
        grid=(b, N_HEADS),
        in_specs=[
            pl.BlockSpec(memory_space=pltpu.SMEM),
            pl.BlockSpec((1, HEAD_W), lambda bi, h: (0, 0)),
            q_spec, kv_spec, kv_spec,
        ],
        out_specs=q_spec,
        out_shape=jax.ShapeDtypeStruct((b, n_q, w), BF16),
        compiler_params=_params("parallel", "parallel"),
        name="attn_sample",
    )(lam, subln_g, q, k_all, v_all)


def _merge_kernel(a_ref, u_ref, gv_ref, x_ref, wm_ref, bs_ref, wo_ref, g_ref, b_ref, y_ref, gated_ref,
                  *, tm, alpha):
    d_a = a_ref.shape[1]
    for c in range(tm // CHUNK_MLP):
        rows = slice(c * CHUNK_MLP, (c + 1) * CHUNK_MLP)
        for grp in range(N_GROUPS):
            cs = slice(grp * GROUP_W, (grp + 1) * GROUP_W)
            mixed = jnp.dot(wm_ref[grp], gv_ref[rows, cs].astype(BF16), preferred_element_type=F32)
            gated_ref[rows, cs] = (u_ref[rows, cs] * (mixed + bs_ref[:, cs])).astype(BF16)
    mix = jnp.dot(a_ref[...], wo_ref[:d_a, :], preferred_element_type=F32)
    mix = mix + jnp.dot(gated_ref[...], wo_ref[d_a:, :], preferred_element_type=F32)
    y_ref[...] = _layer_norm(alpha * x_ref[...] + mix, g_ref[...], b_ref[...])


def _merge(a, u, gvn, x, wm, bs, w_out, ln_g, ln_b, alpha, tm):
    t, d = x.shape
    d_a, d_b = a.shape[1], u.shape[1]
    assert t % tm == 0 and tm % CHUNK_MLP == 0
    row = lambda w: pl.BlockSpec((tm, w), lambda i: (i, 0))
    const = lambda shape: pl.BlockSpec(shape, lambda i: (0,) * len(shape))
    return pl.pallas_call(
        functools.partial(_merge_kernel, tm=tm, alpha=alpha),
        grid=(t // tm,),
        in_specs=[row(d_a), row(d_b), row(d_b), row(d), const(wm.shape), const(bs.shape),
                  const(w_out.shape), const((1, d)), const((1, d))],
        out_specs=row(d),
        out_shape=jax.ShapeDtypeStruct((t, d), F32),
        scratch_shapes=[pltpu.VMEM((tm, d_b), BF16)],
        compiler_params=_params("parallel"),
        name="merge",
    )(a, u, gvn, x, wm, bs, w_out, ln_g, ln_b)


def _ff_chunks(d_ff, width):
    return [(s, min(width, d_ff - s)) for s in range(0, d_ff, width)]


def _swiglu_tile(x, wg_ref, wu_ref, wd_ref, ff_chunk):
    acc = None
    for start, size in _ff_chunks(wg_ref.shape[1], ff_chunk):
        hg = jnp.dot(x, wg_ref[:, start:start + size], preferred_element_type=F32)
        hu = jnp.dot(x, wu_ref[:, start:start + size], preferred_element_type=F32)
        h = (jax.nn.silu(hg) * hu).astype(BF16)
        part = jnp.dot(h, wd_ref[start:start + size, :], preferred_element_type=F32)
        acc = part if acc is None else acc + part
    return acc


def _ffn_dense_kernel(x_ref, wg_ref, wu_ref, wd_ref, g_ref, b_ref, y_ref, *, alpha, ff_chunk):
    x = x_ref[...]
    ffn = _swiglu_tile(x.astype(BF16), wg_ref, wu_ref, wd_ref, ff_chunk)
    y_ref[...] = _layer_norm(alpha * x + ffn, g_ref[...], b_ref[...])


def _ffn_dense(x, wg, wu, wd, ln_g, ln_b, alpha, tm, ff_chunk):
    t, d = x.shape
    assert t % tm == 0
    row = pl.BlockSpec((tm, d), lambda i: (i, 0))
    const = lambda shape: pl.BlockSpec(shape, lambda i: (0, 0), pipeline_mode=pl.Buffered(1))
    return pl.pallas_call(
        functools.partial(_ffn_dense_kernel, alpha=alpha, ff_chunk=ff_chunk),
        grid=(t // tm,),
        in_specs=[row, const(wg.shape), const(wu.shape), const(wd.shape), const((1, d)), const((1, d))],
        out_specs=row,
        out_shape=jax.ShapeDtypeStruct((t, d), F32),
        compiler_params=_params("parallel"),
        name="ffn_dense",
    )(x, wg, wu, wd, ln_g, ln_b)


def _ffn_grouped_kernel(te_ref, nt_ref, x_ref, wg_ref, wu_ref, wd_ref, y_ref, *, ff_chunk):
    @pl.when(pl.program_id(0) < nt_ref[0])
    def _():
        y_ref[...] = _swiglu_tile(x_ref[...], wg_ref, wu_ref, wd_ref, ff_chunk)


def _ffn_grouped(xs, tile_expert, n_tiles_used, wg, wu, wd, tm, ff_chunk):
    r, d = xs.shape
    assert r % tm == 0
    row = pl.BlockSpec((tm, d), lambda i, te, nt: (i, 0))
    expert = lambda shape: pl.BlockSpec((None,) + shape[1:], lambda i, te, nt: (te[i], 0, 0),
                                        pipeline_mode=pl.Buffered(1))
    return pl.pallas_call(
        functools.partial(_ffn_grouped_kernel, ff_chunk=ff_chunk),
        grid_spec=pltpu.PrefetchScalarGridSpec(
            num_scalar_prefetch=2,
            grid=(r // tm,),
            in_specs=[row, expert(wg.shape), expert(wu.shape), expert(wd.shape)],
            out_specs=row,
        ),
        out_shape=jax.ShapeDtypeStruct((r, d), F32),
        compiler_params=_params("arbitrary"),
        name="ffn_grouped",
    )(tile_expert, n_tiles_used, xs, wg, wu, wd)


def _router_kernel(x_ref, wr_ref, e_ref, g_ref):
    logits = lax.dot_general(wr_ref[...], x_ref[...], (((1,), (1,)), ((), ())),
                             precision=lax.Precision.HIGHEST, preferred_element_type=F32)
    idx = lax.broadcasted_iota(jnp.int32, logits.shape, 0)
    m1 = jnp.max(logits, axis=0, keepdims=True)
    i1 = jnp.min(jnp.where(logits == m1, idx, N_EXPERTS), axis=0, keepdims=True)
    rest = jnp.where(idx == i1, -jnp.inf, logits)
    m2 = jnp.max(rest, axis=0, keepdims=True)
    i2 = jnp.min(jnp.where(rest == m2, idx, N_EXPERTS), axis=0, keepdims=True)
    e2 = jnp.exp(m2 - m1)
    denom = 1.0 + e2
    e_ref[...] = jnp.concatenate([i1, i2], axis=0)
    g_ref[...] = jnp.concatenate([1.0 / denom, e2 / denom], axis=0)


def _router(x, w_router_t, tm):
    t, d = x.shape
    assert t % tm == 0
    return pl.pallas_call(
        _router_kernel,
        grid=(t // tm,),
        in_specs=[pl.BlockSpec((tm, d), lambda i: (i, 0)), pl.BlockSpec(w_router_t.shape, lambda i: (0, 0))],
        out_specs=[pl.BlockSpec((2, tm), lambda i: (0, i)), pl.BlockSpec((2, tm), lambda i: (0, i))],
        out_shape=[jax.ShapeDtypeStruct((2, t), jnp.int32), jax.ShapeDtypeStruct((2, t), F32)],
        compiler_params=_params("parallel"),
        name="router",
    )(x, w_router_t)


def _combine_kernel(x_ref, y1_ref, y2_ref, g1_ref, g2_ref, lg_ref, lb_ref, o_ref, *, alpha):
    ffn = g1_ref[...] * y1_ref[...] + g2_ref[...] * y2_ref[...]
    o_ref[...] = _layer_norm(alpha * x_ref[...] + ffn, lg_ref[...], lb_ref[...])


def _combine(x, y1, y2, g1, g2, ln_g, ln_b, alpha, tm):
    t, d = x.shape
    row = pl.BlockSpec((tm, d), lambda i: (i, 0))
    col = pl.BlockSpec((tm, 1), lambda i: (i, 0))
    const = pl.BlockSpec((1, d), lambda i: (0, 0))
    return pl.pallas_call(
        functools.partial(_combine_kernel, alpha=alpha),
        grid=(t // tm,),
        in_specs=[row, row, row, col, col, const, const],
        out_specs=row,
        out_shape=jax.ShapeDtypeStruct((t, d), F32),
        compiler_params=_params("parallel"),
        name="combine",
    )(x, y1, y2, g1, g2, ln_g, ln_b)


def _moe(x, w_router_t, wg, wu, wd, ln_g, ln_b, alpha, tm_route, tm_ffn, ff_chunk):
    t, d = x.shape
    experts, gates = _router(x, w_router_t, tm_route)
    pair_e = experts.reshape(2 * t)
    onehot = (pair_e[:, None] == jnp.arange(N_EXPERTS, dtype=jnp.int32)[None, :]).astype(jnp.int32)
    csum = jnp.cumsum(onehot, axis=0)
    rank = jnp.take_along_axis(csum, pair_e[:, None], axis=1)[:, 0] - 1
    counts = csum[-1]
    tiles_per = (counts + tm_ffn - 1) // tm_ffn
    tile_end = jnp.cumsum(tiles_per)
    starts = (tile_end - tiles_per) * tm_ffn
    pos = starts[pair_e] + rank
    n_tiles = (2 * t) // tm_ffn + N_EXPERTS
    src = jnp.zeros((n_tiles * tm_ffn,), jnp.int32).at[pos].set(jnp.tile(jnp.arange(t, dtype=jnp.int32), 2))
    tile_ids = jnp.arange(n_tiles, dtype=jnp.int32)
    tile_expert = jnp.minimum(jnp.sum((tile_end[None, :] <= tile_ids[:, None]).astype(jnp.int32), axis=1),
                              N_EXPERTS - 1)
    xs = jnp.take(x.astype(BF16), src, axis=0)
    ys = _ffn_grouped(xs, tile_expert, tile_end[-1:].astype(jnp.int32), wg, wu, wd, tm_ffn, ff_chunk)
    y1 = jnp.take(ys, pos[:t], axis=0)
    y2 = jnp.take(ys, pos[t:], axis=0)
    return _combine(x, y1, y2, gates[0][:, None], gates[1][:, None], ln_g, ln_b, alpha, tm_route)


def _spatial_params(w_s, b_s, n_pos):
    reps = CHUNK_MLP // n_pos
    wm = w_s[:, :n_pos, :n_pos] * jnp.tril(jnp.ones((n_pos, n_pos), w_s.dtype))
    eye = jnp.eye(reps, dtype=w_s.dtype)
    wm = jnp.einsum("ab,gij->gaibj", eye, wm).reshape(N_GROUPS, CHUNK_MLP, CHUNK_MLP)
    bias = jnp.tile(b_s[:, :n_pos].T, (reps, 1))
    return wm.astype(BF16), jnp.repeat(bias, GROUP_W, axis=1).astype(F32)


def kernel(x_prompt, x_sample, cache_k, cache_v, w_in, lam_q1, lam_k1, lam_q2, lam_k2, subln_g,
           sgu_ln_g, sgu_ln_b, w_spatial, b_spatial, w_out, ln_mix_g, ln_mix_b,
           w_ffn_gate, w_ffn_up, w_ffn_down, w_router, w_moe_gate, w_moe_up, w_moe_down,
           ln_ffn_g, ln_ffn_b):
    depth = w_in.shape[0]
    batch, seq, d_model = x_prompt.shape
    dec_batch, dec_seq, _ = x_sample.shape
    past_len = cache_k.shape[2]
    alpha = (2.0 * depth) ** 0.25
    t_p, t_s = batch * seq, dec_batch * dec_seq
    n_keys = past_len + dec_seq
    keys_pad = -(-n_keys // LANES) * LANES

    tm_p = min(512, t_p)
    tm_s = min(512, t_s)
    tq = min(512, seq)

    xp = x_prompt.reshape(t_p, d_model)
    xs = x_sample.reshape(t_s, d_model)
    outs = {name: [] for name in ("kp", "vp", "ks", "vs", "gvs")}
    row2 = lambda a: a.reshape(1, -1).astype(F32)

    for l in range(depth):
        lam0 = _lambda_init(l)
        lam = (jnp.exp(jnp.sum(lam_q1[l].astype(F32) * lam_k1[l].astype(F32)))
               - jnp.exp(jnp.sum(lam_q2[l].astype(F32) * lam_k2[l].astype(F32))) + lam0).reshape(1)
        w_in_l = w_in[l].astype(BF16)
        w_out_l = w_out[l].astype(BF16)
        sub_g = row2(subln_g[l])
        sgu_g, sgu_b = row2(sgu_ln_g[l]), row2(sgu_ln_b[l])
        mix_g, mix_b = row2(ln_mix_g[l]), row2(ln_mix_b[l])
        ffn_g, ffn_b = row2(ln_ffn_g[l]), row2(ln_ffn_b[l])
        wm_p, bs_p = _spatial_params(w_spatial[l], b_spatial[l], CHUNK_MLP)
        wm_s, bs_s = _spatial_params(w_spatial[l], b_spatial[l], dec_seq)

        def channel_mix(x, tm):
            i = l // 2
            if l % 2 == 0:
                return _ffn_dense(x, w_ffn_gate[i].astype(BF16), w_ffn_up[i].astype(BF16),
                                  w_ffn_down[i].astype(BF16), ffn_g, ffn_b, alpha, tm, 512)
            return _moe(x, w_router[i].T.astype(F32), w_moe_gate[i].astype(BF16), w_moe_up[i].astype(BF16),
                        w_moe_down[i].astype(BF16), ffn_g, ffn_b, alpha, tm, min(512, 2 * x.shape[0]), 512)

        q, k, v, kb, vb, u, gvn = _proj(xp, w_in_l, sgu_g, sgu_b, tm_p)
        shape3 = (batch, seq, -1)
        a = _attn_prompt(q.reshape(shape3), kb.reshape(shape3), vb.reshape(shape3), lam, sub_g, lam0, tq)
        xp = _merge(a.reshape(t_p, -1), u, gvn, xp, wm_p, bs_p, w_out_l, mix_g, mix_b, alpha, tm_p)
        xp = channel_mix(xp, tm_p)
        outs["kp"].append(k.reshape(batch, seq, N_HEADS, 2, D_HEAD))
        outs["vp"].append(v.reshape(batch, seq, N_HEADS, HEAD_W))

        q, k, v, kb, vb, u, gvn = _proj(xs, w_in_l, sgu_g, sgu_b, tm_s)
        pad = lambda c, new: jnp.pad(
            jnp.concatenate([c.reshape(dec_batch, past_len, -1).astype(BF16),
                             new.reshape(dec_batch, dec_seq, -1)], axis=1),
            ((0, 0), (0, keys_pad - n_keys), (0, 0)))
        a = _attn_sample(q.reshape(dec_batch, dec_seq, -1), pad(cache_k[l], kb), pad(cache_v[l], vb),
                         n_keys, lam, sub_g, lam0)
        xs = _merge(a.reshape(t_s, -1), u, gvn, xs, wm_s, bs_s, w_out_l, mix_g, mix_b, alpha, tm_s)
        xs = channel_mix(xs, tm_s)
        outs["ks"].append(k.reshape(dec_batch, dec_seq, N_HEADS, 2, D_HEAD))
        outs["vs"].append(v.reshape(dec_batch, dec_seq, N_HEADS, HEAD_W))
        outs["gvs"].append(gvn.reshape(dec_batch, dec_seq, -1))

    stack = lambda name: jnp.stack(outs[name], axis=0)
    return (xp.reshape(batch, seq, d_model), xs.reshape(dec_batch, dec_seq, d_model),
            stack("kp"), stack("vp"), stack("ks"), stack("vs"), stack("gvs"))
```

```python
import functools
import math

import jax
import jax.numpy as jnp
from jax import lax
from jax.experimental import pallas as pl
from jax.experimental.pallas import tpu as pltpu

F32 = jnp.float32
BF16 = jnp.bfloat16

N_HEADS = 4
HEAD_W = 128
D_HEAD = 64
N_GROUPS = 4
GROUP_W = 128
CHUNK = 64
CHUNK_MLP = 128
N_EXPERTS = 8
LN_EPS = 1e-5
NEG_INF = -1e30

LANES = 128
MXU_DIM = 256
VMEM_LIMIT_BYTES = 56 * 1024 * 1024


def _lambda_init(layer):
    return 0.8 - 0.6 * math.exp(-0.3 * layer)


def _layer_norm(x, g, b):
    mu = jnp.mean(x, axis=-1, keepdims=True)
    xc = x - mu
    var = jnp.mean(xc * xc, axis=-1, keepdims=True)
    return xc * lax.rsqrt(var + LN_EPS) * g + b


def _params(*semantics):
    return pltpu.CompilerParams(dimension_semantics=semantics, vmem_limit_bytes=VMEM_LIMIT_BYTES)


def _proj_kernel(x_ref, w_ref, g_ref, b_ref, q_ref, k_ref, v_ref, kb_ref, vb_ref, u_ref, gv_ref,
                 *, qk_w, d_a, d_b):
    x = x_ref[...].astype(BF16)

    def cols(start, width):
        return jnp.dot(x, w_ref[:, start:start + width], preferred_element_type=F32)

    q_ref[...] = (cols(0, qk_w) * (D_HEAD ** -0.5)).astype(BF16)
    k = cols(qk_w, qk_w)
    k_ref[...] = k
    kb_ref[...] = k.astype(BF16)
    v = cols(2 * qk_w, d_a)
    v_ref[...] = v
    vb_ref[...] = v.astype(BF16)
    off = 2 * qk_w + d_a
    u_ref[...] = jax.nn.gelu(cols(off, d_b))
    gv = jax.nn.gelu(cols(off + d_b, d_b))
    gv_ref[...] = _layer_norm(gv, g_ref[...], b_ref[...])


def _proj(x, w_in, ln_g, ln_b, tm):
    t, d = x.shape
    qk_w = N_HEADS * HEAD_W
    d_a = N_HEADS * HEAD_W
    d_b = N_GROUPS * GROUP_W
    assert w_in.shape == (d, 2 * qk_w + d_a + 2 * d_b) and t % tm == 0
    row = lambda w: pl.BlockSpec((tm, w), lambda i: (i, 0))
    const = lambda shape: pl.BlockSpec(shape, lambda i: (0, 0))
    return pl.pallas_call(
        functools.partial(_proj_kernel, qk_w=qk_w, d_a=d_a, d_b=d_b),
        grid=(t // tm,),
        in_specs=[row(d), const(w_in.shape), const((1, d_b)), const((1, d_b))],
        out_specs=[row(qk_w), row(qk_w), row(d_a), row(qk_w), row(d_a), row(d_b), row(d_b)],
        out_shape=[
            jax.ShapeDtypeStruct((t, qk_w), BF16),
            jax.ShapeDtypeStruct((t, qk_w), F32),
            jax.ShapeDtypeStruct((t, d_a), F32),
            jax.ShapeDtypeStruct((t, qk_w), BF16),
            jax.ShapeDtypeStruct((t, d_a), BF16),
            jax.ShapeDtypeStruct((t, d_b), F32),
            jax.ShapeDtypeStruct((t, d_b), F32),
        ],
        compiler_params=_params("parallel"),
        name="proj",
    )(x, w_in, ln_g, ln_b)


def _stack_queries(q):
    lane = lax.broadcasted_iota(jnp.int32, q.shape, 1)
    zero = jnp.zeros_like(q)
    return jnp.concatenate([jnp.where(lane < D_HEAD, q, zero), jnp.where(lane >= D_HEAD, q, zero)], axis=0)


def _scores(qs, kb):
    return lax.dot_general(qs, kb, (((1,), (1,)), ((), ())), preferred_element_type=F32)


def _diff_merge(o1, o2, lam, g, lam0):
    o = o1 - lam * o2
    ms = jnp.mean(o * o, axis=-1, keepdims=True)
    return o * lax.rsqrt(ms + LN_EPS) * g * (1.0 - lam0)


def _attn_prompt_kernel(lam_ref, g_ref, q_ref, k_ref, v_ref, o_ref,
                        qs_ref, s_ref, p_ref, a_ref, m_ref, acc_ref, *, tq, lam0):
    i = pl.program_id(2)
    n_rep = tq // LANES
    qs_ref[...] = _stack_queries(q_ref[...])
    m_ref[...] = jnp.full(m_ref.shape, NEG_INF, F32)
    acc_ref[...] = jnp.zeros(acc_ref.shape, F32)
    p_ref[1] = jnp.zeros(p_ref.shape[1:], BF16)
    a_ref[1] = jnp.ones(a_ref.shape[1:], F32)
    ones = jnp.ones((tq, HEAD_W), BF16)

    def scores(j, slot):
        off = pl.multiple_of(j * tq, tq)
        s_ref[slot] = _scores(qs_ref[...], k_ref[pl.ds(off, tq), :])

    def softmax(slot, masked, out_slot=None):
        out_slot = slot if out_slot is None else out_slot
        for r0 in range(0, 2 * tq, CHUNK):
            rows = slice(r0, r0 + CHUNK)
            s = s_ref[slot, rows, :]
            if masked:
                n_vis = (r0 % tq) + CHUNK
                s = jnp.where(lax.broadcasted_iota(jnp.int32, s.shape, 1) < n_vis, s, NEG_INF)
            m_prev = m_ref[rows, :]
            m_new = jnp.maximum(m_prev, jnp.max(s, axis=1, keepdims=True))
            a_ref[out_slot, rows, :] = jnp.exp(m_prev - m_new)
            p_ref[out_slot, rows, :] = jnp.exp(s - jnp.concatenate([m_new] * n_rep, axis=1)).astype(BF16)
            m_ref[rows, :] = m_new

    def pv(j, slot):
        off = pl.multiple_of(j * tq, tq)
        v_ext = jnp.concatenate([v_ref[pl.ds(off, tq), :], ones], axis=1)
        a = a_ref[slot]
        acc_ref[...] = (jnp.concatenate([a, a], axis=1) * acc_ref[...]
                        + jnp.dot(p_ref[slot], v_ext, preferred_element_type=F32))

    def step(t, slot):
        pv(jnp.maximum(t - 1, 0), 1 - slot)
        softmax(slot, False)
        scores(t + 1, 1 - slot)

    def last_step(t, slot, out_slot):
        pv(jnp.maximum(t - 1, 0), 1 - slot)
        softmax(slot, True, out_slot)
        pv(t, out_slot)

    scores(0, 0)

    def pair(u, carry):
        step(2 * u, 0)
        step(2 * u + 1, 1)
        return carry

    lax.fori_loop(0, i // 2, pair, 0)

    @pl.when(i % 2 == 0)
    def _():
        last_step(i, 0, 0)

    @pl.when(i % 2 == 1)
    def _():
        step(i - 1, 0)
        last_step(i, 1, 2)

    acc = acc_ref[...]
    o1 = acc[:tq, :HEAD_W] / acc[:tq, HEAD_W:]
    o2 = acc[tq:, :HEAD_W] / acc[tq:, HEAD_W:]
    o_ref[...] = _diff_merge(o1, o2, lam_ref[0], g_ref[...], lam0).astype(o_ref.dtype)


def _attn_prompt(q, k, v, lam, subln_g, lam0, tq):
    b, s, w = q.shape
    assert s % tq == 0 and tq % CHUNK == 0 and w == N_HEADS * HEAD_W
    q_spec = pl.BlockSpec((None, tq, HEAD_W), lambda bi, h, i: (bi, i, h))
    kv_spec = pl.BlockSpec((None, s, HEAD_W), lambda bi, h, i: (bi, 0, h))
    return pl.pallas_call(
        functools.partial(_attn_prompt_kernel, tq=tq, lam0=lam0),
        grid=(b, N_HEADS, s // tq),
        in_specs=[
            pl.BlockSpec(memory_space=pltpu.SMEM),
            pl.BlockSpec((1, HEAD_W), lambda bi, h, i: (0, 0)),
            q_spec, kv_spec, kv_spec,
        ],
        out_specs=q_spec,
        out_shape=jax.ShapeDtypeStruct((b, s, w), BF16),
        scratch_shapes=[
            pltpu.VMEM((2 * tq, HEAD_W), BF16),
            pltpu.VMEM((2, 2 * tq, tq), F32),
            pltpu.VMEM((3, 2 * tq, tq), BF16),
            pltpu.VMEM((3, 2 * tq, LANES), F32),
            pltpu.VMEM((2 * tq, LANES), F32),
            pltpu.VMEM((2 * tq, 2 * HEAD_W), F32),
        ],
        compiler_params=_params("parallel", "parallel", "arbitrary"),
        name="attn_prompt",
    )(lam, subln_g, q, k, v)


def _attn_sample_kernel(lam_ref, g_ref, q_ref, k_ref, v_ref, o_ref, *, n_q, n_keys, lam0):
    qs = _stack_queries(q_ref[...])
    s = _scores(qs, k_ref[...])
    col = lax.broadcasted_iota(jnp.int32, s.shape, 1)
    s = jnp.where(col < n_keys, s, NEG_INF)
    m = jnp.max(s, axis=1, keepdims=True)
    p = jnp.exp(s - m)
    l = jnp.sum(p, axis=1, keepdims=True)
    o = jnp.dot(p.astype(BF16), v_ref[...], preferred_element_type=F32) / l
    o_ref[...] = _diff_merge(o[:n_q], o[n_q:], lam_ref[0], g_ref[...], lam0).astype(o_ref.dtype)


def _attn_sample(q, k_all, v_all, n_keys, lam, subln_g, lam0):
    b, n_q, w = q.shape
    n_pad = k_all.shape[1]
    q_spec = pl.BlockSpec((None, n_q, HEAD_W), lambda bi, h: (bi, 0, h))
    kv_spec = pl.BlockSpec((None, n_pad, HEAD_W), lambda bi, h: (bi, 0, h))
    return pl.pallas_call(
        functools.partial(_attn_sample_kernel, n_q=n_q, n_keys=n_keys, lam0=lam0),
        grid=(b, N_HEADS),
        in_specs=[
            pl.BlockSpec(memory_space=pltpu.SMEM),
            pl.BlockSpec((1, HEAD_W), lambda bi, h: (0, 0)),
            q_spec, kv_spec, kv_spec,
        ],
        out_specs=q_spec,
        out_shape=jax.ShapeDtypeStruct((b, n_q, w), BF16),
        compiler_params=_params("parallel", "parallel"),
        name="attn_sample",
    )(lam, subln_g, q, k_all, v_all)


def _merge_kernel(a_ref, u_ref, gv_ref, x_ref, wm_ref, bs_ref, wo_ref, g_ref, b_ref, y_ref, gated_ref,
                  *, tm, alpha):
    d_a = a_ref.shape[1]
    for c in range(tm // CHUNK_MLP):
        rows = slice(c * CHUNK_MLP, (c + 1) * CHUNK_MLP)
        for grp in range(N_GROUPS):
            cs = slice(grp * GROUP_W, (grp + 1) * GROUP_W)
            mixed = jnp.dot(wm_ref[grp], gv_ref[rows, cs].astype(BF16), preferred_element_type=F32)
            gated_ref[rows, cs] = (u_ref[rows, cs] * (mixed + bs_ref[:, cs])).astype(BF16)
    mix = jnp.dot(a_ref[...], wo_ref[:d_a, :], preferred_element_type=F32)
    mix = mix + jnp.dot(gated_ref[...], wo_ref[d_a:, :], preferred_element_type=F32)
    y_ref[...] = _layer_norm(alpha * x_ref[...] + mix, g_ref[...], b_ref[...])


def _merge(a, u, gvn, x, wm, bs, w_out, ln_g, ln_b, alpha, tm):
    t, d = x.shape
    d_a, d_b = a.shape[1], u.shape[1]
    assert t % tm == 0 and tm % CHUNK_MLP == 0
    row = lambda w: pl.BlockSpec((tm, w), lambda i: (i, 0))
    const = lambda shape: pl.BlockSpec(shape, lambda i: (0,) * len(shape))
    return pl.pallas_call(
        functools.partial(_merge_kernel, tm=tm, alpha=alpha),
        grid=(t // tm,),
        in_specs=[row(d_a), row(d_b), row(d_b), row(d), const(wm.shape), const(bs.shape),
                  const(w_out.shape), const((1, d)), const((1, d))],
        out_specs=row(d),
        out_shape=jax.ShapeDtypeStruct((t, d), F32),
        scratch_shapes=[pltpu.VMEM((tm, d_b), BF16)],
        compiler_params=_params("parallel"),
        name="merge",
    )(a, u, gvn, x, wm, bs, w_out, ln_g, ln_b)


def _ff_chunks(d_ff, width):
    return [(s, min(width, d_ff - s)) for s in range(0, d_ff, width)]


def _swiglu_tile(x, wg_ref, wu_ref, wd_ref, ff_chunk):
    acc = None
    for start, size in _ff_chunks(wg_ref.shape[1], ff_chunk):
        hg = jnp.dot(x, wg_ref[:, start:start + size], preferred_element_type=F32)
        hu = jnp.dot(x, wu_ref[:, start:start + size], preferred_element_type=F32)
        h = (jax.nn.silu(hg) * hu).astype(BF16)
        part = jnp.dot(h, wd_ref[start:start + size, :], preferred_element_type=F32)
        acc = part if acc is None else acc + part
    return acc


def _ffn_dense_kernel(x_ref, wg_ref, wu_ref, wd_ref, g_ref, b_ref, y_ref, *, alpha, ff_chunk):
    x = x_ref[...]
    ffn = _swiglu_tile(x.astype(BF16), wg_ref, wu_ref, wd_ref, ff_chunk)
    y_ref[...] = _layer_norm(alpha * x + ffn, g_ref[...], b_ref[...])


def _ffn_dense(x, wg, wu, wd, ln_g, ln_b, alpha, tm, ff_chunk):
    t, d = x.shape
    assert t % tm == 0
    row = pl.BlockSpec((tm, d), lambda i: (i, 0))
    const = lambda shape: pl.BlockSpec(shape, lambda i: (0, 0), pipeline_mode=pl.Buffered(1))
    return pl.pallas_call(
        functools.partial(_ffn_dense_kernel, alpha=alpha, ff_chunk=ff_chunk),
        grid=(t // tm,),
        in_specs=[row, const(wg.shape), const(wu.shape), const(wd.shape), const((1, d)), const((1, d))],
        out_specs=row,
        out_shape=jax.ShapeDtypeStruct((t, d), F32),
        compiler_params=_params("parallel"),
        name="ffn_dense",
    )(x, wg, wu, wd, ln_g, ln_b)


def _ffn_grouped_kernel(te_ref, nt_ref, x_ref, wg_ref, wu_ref, wd_ref, y_ref, *, ff_chunk):
    @pl.when(pl.program_id(0) < nt_ref[0])
    def _():
        y_ref[...] = _swiglu_tile(x_ref[...], wg_ref, wu_ref, wd_ref, ff_chunk)


def _ffn_grouped(xs, tile_expert, n_tiles_used, wg, wu, wd, tm, ff_chunk):
    r, d = xs.shape
    assert r % tm == 0
    row = pl.BlockSpec((tm, d), lambda i, te, nt: (i, 0))
    expert = lambda shape: pl.BlockSpec((None,) + shape[1:], lambda i, te, nt: (te[i], 0, 0),
                                        pipeline_mode=pl.Buffered(1))
    return pl.pallas_call(
        functools.partial(_ffn_grouped_kernel, ff_chunk=ff_chunk),
        grid_spec=pltpu.PrefetchScalarGridSpec(
            num_scalar_prefetch=2,
            grid=(r // tm,),
            in_specs=[row, expert(wg.shape), expert(wu.shape), expert(wd.shape)],
            out_specs=row,
        ),
        out_shape=jax.ShapeDtypeStruct((r, d), F32),
        compiler_params=_params("arbitrary"),
        name="ffn_grouped",
    )(tile_expert, n_tiles_used, xs, wg, wu, wd)


def _router_kernel(x_ref, wr_ref, e_ref, g_ref):
    logits = lax.dot_general(wr_ref[...], x_ref[...], (((1,), (1,)), ((), ())),
                             precision=lax.Precision.HIGHEST, preferred_element_type=F32)
    idx = lax.broadcasted_iota(jnp.int32, logits.shape, 0)
    m1 = jnp.max(logits, axis=0, keepdims=True)
    i1 = jnp.min(jnp.where(logits == m1, idx, N_EXPERTS), axis=0, keepdims=True)
    rest = jnp.where(idx == i1, -jnp.inf, logits)
    m2 = jnp.max(rest, axis=0, keepdims=True)
    i2 = jnp.min(jnp.where(rest == m2, idx, N_EXPERTS), axis=0, keepdims=True)
    e2 = jnp.exp(m2 - m1)
    denom = 1.0 + e2
    e_ref[...] = jnp.concatenate([i1, i2], axis=0)
    g_ref[...] = jnp.concatenate([1.0 / denom, e2 / denom], axis=0)


def _router(x, w_router_t, tm):
    t, d = x.shape
    assert t % tm == 0
    return pl.pallas_call(
        _router_kernel,
        grid=(t // tm,),
        in_specs=[pl.BlockSpec((tm, d), lambda i: (i, 0)), pl.BlockSpec(w_router_t.shape, lambda i: (0, 0))],
        out_specs=[pl.BlockSpec((2, tm), lambda i: (0, i)), pl.BlockSpec((2, tm), lambda i: (0, i))],
        out_shape=[jax.ShapeDtypeStruct((2, t), jnp.int32), jax.ShapeDtypeStruct((2, t), F32)],
        compiler_params=_params("parallel"),
        name="router",
    )(x, w_router_t)


def _combine_kernel(x_ref, y1_ref, y2_ref, g1_ref, g2_ref, lg_ref, lb_ref, o_ref, *, alpha):
    ffn = g1_ref[...] * y1_ref[...] + g2_ref[...] * y2_ref[...]
    o_ref[...] = _layer_norm(alpha * x_ref[...] + ffn, lg_ref[...], lb_ref[...])


def _combine(x, y1, y2, g1, g2, ln_g, ln_b, alpha, tm):
    t, d = x.shape
    row = pl.BlockSpec((tm, d), lambda i: (i, 0))
    col = pl.BlockSpec((tm, 1), lambda i: (i, 0))
    const = pl.BlockSpec((1, d), lambda i: (0, 0))
    return pl.pallas_call(
        functools.partial(_combine_kernel, alpha=alpha),
        grid=(t // tm,),
        in_specs=[row, row, row, col, col, const, const],
        out_specs=row,
        out_shape=jax.ShapeDtypeStruct((t, d), F32),
        compiler_params=_params("parallel"),
        name="combine",
    )(x, y1, y2, g1, g2, ln_g, ln_b)


def _moe(x, w_router_t, wg, wu, wd, ln_g, ln_b, alpha, tm_route, tm_ffn, ff_chunk):
    t, d = x.shape
    experts, gates = _router(x, w_router_t, tm_route)
    pair_e = experts.reshape(2 * t)
    onehot = (pair_e[:, None] == jnp.arange(N_EXPERTS, dtype=jnp.int32)[None, :]).astype(jnp.int32)
    csum = jnp.cumsum(onehot, axis=0)
    rank = jnp.take_along_axis(csum, pair_e[:, None], axis=1)[:, 0] - 1
    counts = csum[-1]
    tiles_per = (counts + tm_ffn - 1) // tm_ffn
    tile_end = jnp.cumsum(tiles_per)
    starts = (tile_end - tiles_per) * tm_ffn
    pos = starts[pair_e] + rank
    n_tiles = (2 * t) // tm_ffn + N_EXPERTS
    src = jnp.zeros((n_tiles * tm_ffn,), jnp.int32).at[pos].set(jnp.tile(jnp.arange(t, dtype=jnp.int32), 2))
    tile_ids = jnp.arange(n_tiles, dtype=jnp.int32)
    tile_expert = jnp.minimum(jnp.sum((tile_end[None, :] <= tile_ids[:, None]).astype(jnp.int32), axis=1),
                              N_EXPERTS - 1)
    xs = jnp.take(x.astype(BF16), src, axis=0)
    ys = _ffn_grouped(xs, tile_expert, tile_end[-1:].astype(jnp.int32), wg, wu, wd, tm_ffn, ff_chunk)
    y1 = jnp.take(ys, pos[:t], axis=0)
    y2 = jnp.take(ys, pos[t:], axis=0)
    return _combine(x, y1, y2, gates[0][:, None], gates[1][:, None], ln_g, ln_b, alpha, tm_route)


def _spatial_params(w_s, b_s, n_pos):
    reps = CHUNK_MLP // n_pos
    wm = w_s[:, :n_pos, :n_pos] * jnp.tril(jnp.ones((n_pos, n_pos), w_s.dtype))
    eye = jnp.eye(reps, dtype=w_s.dtype)
    wm = jnp.einsum("ab,gij->gaibj", eye, wm).reshape(N_GROUPS, CHUNK_MLP, CHUNK_MLP)
    bias = jnp.tile(b_s[:, :n_pos].T, (reps, 1))
    return wm.astype(BF16), jnp.repeat(bias, GROUP_W, axis=1).astype(F32)


def kernel(x_prompt, x_sample, cache_k, cache_v, w_in, lam_q1, lam_k1, lam_q2, lam_k2, subln_g,
           sgu_ln_g, sgu_ln_b, w_spatial, b_spatial, w_out, ln_mix_g, ln_mix_b,
           w_ffn_gate, w_ffn_up, w_ffn_down, w_router, w_moe_gate, w_moe_up, w_moe_down,
           ln_ffn_g, ln_ffn_b):
    depth = w_in.shape[0]
    batch, seq, d_model = x_prompt.shape
    dec_batch, dec_seq, _ = x_sample.shape
    past_len = cache_k.shape[2]
    alpha = (2.0 * depth) ** 0.25
    t_p, t_s = batch * seq, dec_batch * dec_seq
    n_keys = past_len + dec_seq
    keys_pad = -(-n_keys // LANES) * LANES

    tm_p = min(512, t_p)
    tm_s = min(512, t_s)
    tq = min(512, seq)

    xp = x_prompt.reshape(t_p, d_model)
    xs = x_sample.reshape(t_s, d_model)
    outs = {name: [] for name in ("kp", "vp", "ks", "vs", "gvs")}
    row2 = lambda a: a.reshape(1, -1).astype(F32)

    for l in range(depth):
        lam0 = _lambda_init(l)
        lam = (jnp.exp(jnp.sum(lam_q1[l].astype(F32) * lam_k1[l].astype(F32)))
               - jnp.exp(jnp.sum(lam_q2[l].astype(F32) * lam_k2[l].astype(F32))) + lam0).reshape(1)
        w_in_l = w_in[l].astype(BF16)
        w_out_l = w_out[l].astype(BF16)
        sub_g = row2(subln_g[l])
        sgu_g, sgu_b = row2(sgu_ln_g[l]), row2(sgu_ln_b[l])
        mix_g, mix_b = row2(ln_mix_g[l]), row2(ln_mix_b[l])
        ffn_g, ffn_b = row2(ln_ffn_g[l]), row2(ln_ffn_b[l])
        wm_p, bs_p = _spatial_params(w_spatial[l], b_spatial[l], CHUNK_MLP)
        wm_s, bs_s = _spatial_params(w_spatial[l], b_spatial[l], dec_seq)

        def channel_mix(x, tm):
            i = l // 2
            if l % 2 == 0:
                return _ffn_dense(x, w_ffn_gate[i].astype(BF16), w_ffn_up[i].astype(BF16),
                                  w_ffn_down[i].astype(BF16), ffn_g, ffn_b, alpha, tm, 512)
            return _moe(x, w_router[i].T.astype(F32), w_moe_gate[i].astype(BF16), w_moe_up[i].astype(BF16),
                        w_moe_down[i].astype(BF16), ffn_g, ffn_b, alpha, tm, min(512, 2 * x.shape[0]), 512)

        q, k, v, kb, vb, u, gvn = _proj(xp, w_in_l, sgu_g, sgu_b, tm_p)
        shape3 = (batch, seq, -1)
        a = _attn_prompt(q.reshape(shape3), kb.reshape(shape3), vb.reshape(shape3), lam, sub_g, lam0, tq)
        xp = _merge(a.reshape(t_p, -1), u, gvn, xp, wm_p, bs_p, w_out_l, mix_g, mix_b, alpha, tm_p)
        xp = channel_mix(xp, tm_p)
        outs["kp"].append(k.reshape(batch, seq, N_HEADS, 2, D_HEAD))
        outs["vp"].append(v.reshape(batch, seq, N_HEADS, HEAD_W))

        q, k, v, kb, vb, u, gvn = _proj(xs, w_in_l, sgu_g, sgu_b, tm_s)
        pad = lambda c, new: jnp.pad(
            jnp.concatenate([c.reshape(dec_batch, past_len, -1).astype(BF16),
                             new.reshape(dec_batch, dec_seq, -1)], axis=1),
            ((0, 0), (0, keys_pad - n_keys), (0, 0)))
        a = _attn_sample(q.reshape(dec_batch, dec_seq, -1), pad(cache_k[l], kb), pad(cache_v[l], vb),
                         n_keys, lam, sub_g, lam0)
        xs = _merge(a.reshape(t_s, -1), u, gvn, xs, wm_s, bs_s, w_out_l, mix_g, mix_b, alpha, tm_s)
        xs = channel_mix(xs, tm_s)
        outs["ks"].append(k.reshape(dec_batch, dec_seq, N_HEADS, 2, D_HEAD))
        outs["vs"].append(v.reshape(dec_batch, dec_seq, N_HEADS, HEAD_W))
        outs["gvs"].append(gvn.reshape(dec_batch, dec_seq, -1))

    stack = lambda name: jnp.stack(outs[name], axis=0)
    return (xp.reshape(batch, seq, d_model), xs.reshape(dec_batch, dec_seq, d_model),
            stack("kp"), stack("vp"), stack("ks"), stack("vs"), stack("gvs"))
```

```python
import functools
import math

import jax
import jax.numpy as jnp
from jax import lax
from jax.experimental import pallas as pl
from jax.experimental.pallas import tpu as pltpu

F32 = jnp.float32
BF16 = jnp.bfloat16

N_HEADS = 4
HEAD_W = 128
D_HEAD = 64
N_GROUPS = 4
GROUP_W = 128
CHUNK = 64
CHUNK_MLP = 128
N_EXPERTS = 8
LN_EPS = 1e-5
LOG2_E = math.log2(math.e)
NEG_INF = -1e30

LANES = 128
MXU_DIM = 256
VMEM_LIMIT_BYTES = 56 * 1024 * 1024


def _lambda_init(layer):
    return 0.8 - 0.6 * math.exp(-0.3 * layer)


def _layer_norm(x, g, b):
    mu = jnp.mean(x, axis=-1, keepdims=True)
    xc = x - mu
    var = jnp.mean(xc * xc, axis=-1, keepdims=True)
    return xc * lax.rsqrt(var + LN_EPS) * g + b


def _params(*semantics):
    return pltpu.CompilerParams(dimension_semantics=semantics, vmem_limit_bytes=VMEM_LIMIT_BYTES)


def _proj_kernel(*refs, qk_w, d_a, d_b, n_alias):
    x_ref, w_ref, g_ref, b_ref = refs[:4]
    q_ref, k_ref, v_ref, kb_ref, vb_ref, u_ref, gv_ref = refs[4 + n_alias:]
    x = x_ref[...].astype(BF16)

    def cols(start, width):
        return jnp.dot(x, w_ref[:, start:start + width], preferred_element_type=F32)

    q_ref[...] = (cols(0, qk_w) * (D_HEAD ** -0.5 * LOG2_E)).astype(BF16)
    k = cols(qk_w, qk_w)
    k_ref[...] = k
    kb_ref[...] = k.astype(BF16)
    v = cols(2 * qk_w, d_a)
    v_ref[...] = v
    vb_ref[...] = v.astype(BF16)
    off = 2 * qk_w + d_a
    u_ref[...] = jax.nn.gelu(cols(off, d_b))
    gv = jax.nn.gelu(cols(off + d_b, d_b))
    gv_ref[...] = _layer_norm(gv, g_ref[...], b_ref[...])


def _proj(x, w_in, ln_g, ln_b, tm, layer, depth, kv_stacks):
    t, d = x.shape
    qk_w = N_HEADS * HEAD_W
    d_a = N_HEADS * HEAD_W
    d_b = N_GROUPS * GROUP_W
    assert w_in.shape == (d, 2 * qk_w + d_a + 2 * d_b) and t % tm == 0
    row = lambda w: pl.BlockSpec((tm, w), lambda i: (i, 0))
    const = lambda shape: pl.BlockSpec(shape, lambda i: (0, 0))
    stack = lambda w: pl.BlockSpec((None, tm, w), lambda i: (layer, i, 0))
    n_alias = len(kv_stacks)
    return pl.pallas_call(
        functools.partial(_proj_kernel, qk_w=qk_w, d_a=d_a, d_b=d_b, n_alias=n_alias),
        grid=(t // tm,),
        in_specs=[row(d), const(w_in.shape), const((1, d_b)), const((1, d_b))]
                 + [pl.BlockSpec(memory_space=pl.ANY)] * n_alias,
        out_specs=[row(qk_w), stack(qk_w), stack(d_a), row(qk_w), row(d_a), row(d_b), row(d_b)],
        out_shape=[
            jax.ShapeDtypeStruct((t, qk_w), BF16),
            jax.ShapeDtypeStruct((depth, t, qk_w), F32),
            jax.ShapeDtypeStruct((depth, t, d_a), F32),
            jax.ShapeDtypeStruct((t, qk_w), BF16),
            jax.ShapeDtypeStruct((t, d_a), BF16),
            jax.ShapeDtypeStruct((t, d_b), F32),
            jax.ShapeDtypeStruct((t, d_b), F32),
        ],
        input_output_aliases={4 + j: 1 + j for j in range(n_alias)},
        compiler_params=_params("parallel"),
        name="proj",
    )(x, w_in, ln_g, ln_b, *kv_stacks)


def _stack_queries(q):
    lane = lax.broadcasted_iota(jnp.int32, q.shape, 1)
    zero = jnp.zeros_like(q)
    return jnp.concatenate([jnp.where(lane < D_HEAD, q, zero), jnp.where(lane >= D_HEAD, q, zero)], axis=0)


def _scores(qs, kb):
    return lax.dot_general(qs, kb, (((1,), (1,)), ((), ())), preferred_element_type=F32)


def _diff_merge(o1, o2, lam, g, lam0):
    o = o1 - lam * o2
    ms = jnp.mean(o * o, axis=-1, keepdims=True)
    return o * lax.rsqrt(ms + LN_EPS) * g * (1.0 - lam0)


def _attn_prompt_kernel(lam_ref, g_ref, q_ref, k_ref, v_ref, o_ref,
                        qs_ref, s_ref, p_ref, a_ref, m_ref, acc_ref, *, tq, lam0):
    i = pl.program_id(2)
    n_rep = tq // LANES
    qs_ref[...] = _stack_queries(q_ref[...])
    m_ref[...] = jnp.full(m_ref.shape, NEG_INF, F32)
    acc_ref[...] = jnp.zeros(acc_ref.shape, F32)
    p_ref[1] = jnp.zeros(p_ref.shape[1:], BF16)
    a_ref[1] = jnp.ones(a_ref.shape[1:], F32)
    ones = jnp.ones((tq, HEAD_W), BF16)

    def scores(j, slot):
        off = pl.multiple_of(j * tq, tq)
        s_ref[slot] = _scores(qs_ref[...], k_ref[pl.ds(off, tq), :])

    def softmax(slot, masked, out_slot=None):
        out_slot = slot if out_slot is None else out_slot
        for r0 in range(0, 2 * tq, CHUNK):
            rows = slice(r0, r0 + CHUNK)
            s = s_ref[slot, rows, :]
            if masked:
                n_vis = (r0 % tq) + CHUNK
                s = jnp.where(lax.broadcasted_iota(jnp.int32, s.shape, 1) < n_vis, s, NEG_INF)
            m_prev = m_ref[rows, :]
            m_new = jnp.maximum(m_prev, jnp.max(s, axis=1, keepdims=True))
            a_ref[out_slot, rows, :] = jnp.exp2(m_prev - m_new)
            p_ref[out_slot, rows, :] = jnp.exp2(s - jnp.concatenate([m_new] * n_rep, axis=1)).astype(BF16)
            m_ref[rows, :] = m_new

    def pv(j, slot):
        off = pl.multiple_of(j * tq, tq)
        v_ext = jnp.concatenate([v_ref[pl.ds(off, tq), :], ones], axis=1)
        a = a_ref[slot]
        acc_ref[...] = (jnp.concatenate([a, a], axis=1) * acc_ref[...]
                        + jnp.dot(p_ref[slot], v_ext, preferred_element_type=F32))

    def step(t, slot):
        scores(t + 1, 1 - slot)
        pv(jnp.maximum(t - 1, 0), 1 - slot)
        softmax(slot, False)

    def last_step(t, slot, out_slot):
        pv(jnp.maximum(t - 1, 0), 1 - slot)
        softmax(slot, True, out_slot)
        pv(t, out_slot)

    scores(0, 0)

    def pair(u, carry):
        step(2 * u, 0)
        step(2 * u + 1, 1)
        return carry

    lax.fori_loop(0, i // 2, pair, 0)

    @pl.when(i % 2 == 0)
    def _():
        last_step(i, 0, 0)

    @pl.when(i % 2 == 1)
    def _():
        step(i - 1, 0)
        last_step(i, 1, 2)

    acc = acc_ref[...]
    o1 = acc[:tq, :HEAD_W] / acc[:tq, HEAD_W:]
    o2 = acc[tq:, :HEAD_W] / acc[tq:, HEAD_W:]
    o_ref[...] = _diff_merge(o1, o2, lam_ref[0], g_ref[...], lam0).astype(o_ref.dtype)


def _attn_prompt(q, k, v, lam, subln_g, lam0, tq):
    b, s, w = q.shape
    assert s % tq == 0 and tq % CHUNK == 0 and w == N_HEADS * HEAD_W
    q_spec = pl.BlockSpec((None, tq, HEAD_W), lambda bi, h, i: (bi, i, h))
    kv_spec = pl.BlockSpec((None, s, HEAD_W), lambda bi, h, i: (bi, 0, h))
    return pl.pallas_call(
        functools.partial(_attn_prompt_kernel, tq=tq, lam0=lam0),
        grid=(b, N_HEADS, s // tq),
        in_specs=[
            pl.BlockSpec(memory_space=pltpu.SMEM),
            pl.BlockSpec((1, HEAD_W), lambda bi, h, i: (0, 0)),
            q_spec, kv_spec, kv_spec,
        ],
        out_specs=q_spec,
        out_shape=jax.ShapeDtypeStruct((b, s, w), BF16),
        scratch_shapes=[
            pltpu.VMEM((2 * tq, HEAD_W), BF16),
            pltpu.VMEM((2, 2 * tq, tq), F32),
            pltpu.VMEM((3, 2 * tq, tq), BF16),
            pltpu.VMEM((3, 2 * tq, LANES), F32),
            pltpu.VMEM((2 * tq, LANES), F32),
            pltpu.VMEM((2 * tq, 2 * HEAD_W), F32),
        ],
        compiler_params=_params("parallel", "parallel", "arbitrary"),
        name="attn_prompt",
    )(lam, subln_g, q, k, v)


def _attn_sample_kernel(lam_ref, g_ref, q_ref, k_ref, v_ref, o_ref, *, n_q, n_keys, lam0):
    qs = _stack_queries(q_ref[...])
    s = _scores(qs, k_ref[...])
    col = lax.broadcasted_iota(jnp.int32, s.shape, 1)
    s = jnp.where(col < n_keys, s, NEG_INF)
    m = jnp.max(s, axis=1, keepdims=True)
    p = jnp.exp2(s - m)
    l = jnp.sum(p, axis=1, keepdims=True)
    o = jnp.dot(p.astype(BF16), v_ref[...], preferred_element_type=F32) / l
    o_ref[...] = _diff_merge(o[:n_q], o[n_q:], lam_ref[0], g_ref[...], lam0).astype(o_ref.dtype)


def _attn_sample(q, k_all, v_all, n_keys, lam, subln_g, lam0):
    b, n_q, w = q.shape
    n_pad = k_all.shape[1]
    q_spec = pl.BlockSpec((None, n_q, HEAD_W), lambda bi, h: (bi, 0, h))
    kv_spec = pl.BlockSpec((None, n_pad, HEAD_W), lambda bi, h: (bi, 0, h))
    return pl.pallas_call(
        functools.partial(_attn_sample_kernel, n_q=n_q, n_keys=n_keys, lam0=lam0),
        grid=(b, N_HEADS),
        in_specs=[
            pl.BlockSpec(memory_space=pltpu.SMEM),
            pl.BlockSpec((1, HEAD_W), lambda bi, h: (0, 0)),
            q_spec, kv_spec, kv_spec,
        ],
        out_specs=q_spec,
        out_shape=jax.ShapeDtypeStruct((b, n_q, w), BF16),
        compiler_params=_params("parallel", "parallel"),
        name="attn_sample",
    )(lam, subln_g, q, k_all, v_all)


def _merge_kernel(a_ref, u_ref, gv_ref, x_ref, wm_ref, bs_ref, wo_ref, g_ref, b_ref, y_ref, gated_ref,
                  *, tm, alpha):
    d_a = a_ref.shape[1]
    for c in range(tm // CHUNK_MLP):
        rows = slice(c * CHUNK_MLP, (c + 1) * CHUNK_MLP)
        for grp in range(N_GROUPS):
            cs = slice(grp * GROUP_W, (grp + 1) * GROUP_W)
            mixed = jnp.dot(wm_ref[grp], gv_ref[rows, cs].astype(BF16), preferred_element_type=F32)
            gated_ref[rows, cs] = (u_ref[rows, cs] * (mixed + bs_ref[:, cs])).astype(BF16)
    mix = jnp.dot(a_ref[...], wo_ref[:d_a, :], preferred_element_type=F32)
    mix = mix + jnp.dot(gated_ref[...], wo_ref[d_a:, :], preferred_element_type=F32)
    y_ref[...] = _layer_norm(alpha * x_ref[...] + mix, g_ref[...], b_ref[...])


def _merge(a, u, gvn, x, wm, bs, w_out, ln_g, ln_b, alpha, tm):
    t, d = x.shape
    d_a, d_b = a.shape[1], u.shape[1]
    assert t % tm == 0 and tm % CHUNK_MLP == 0
    row = lambda w: pl.BlockSpec((tm, w), lambda i: (i, 0))
    const = lambda shape: pl.BlockSpec(shape, lambda i: (0,) * len(shape))
    return pl.pallas_call(
        functools.partial(_merge_kernel, tm=tm, alpha=alpha),
        grid=(t // tm,),
        in_specs=[row(d_a), row(d_b), row(d_b), row(d), const(wm.shape), const(bs.shape),
                  const(w_out.shape), const((1, d)), const((1, d))],
        out_specs=row(d),
        out_shape=jax.ShapeDtypeStruct((t, d), F32),
        scratch_shapes=[pltpu.VMEM((tm, d_b), BF16)],
        compiler_params=_params("parallel"),
        name="merge",
    )(a, u, gvn, x, wm, bs, w_out, ln_g, ln_b)


def _ff_chunks(d_ff, width):
    return [(s, min(width, d_ff - s)) for s in range(0, d_ff, width)]


def _swiglu_tile(x, wg_ref, wu_ref, wd_ref, ff_chunk):
    acc = None
    for start, size in _ff_chunks(wg_ref.shape[1], ff_chunk):
        hg = jnp.dot(x, wg_ref[:, start:start + size], preferred_element_type=F32)
        hu = jnp.dot(x, wu_ref[:, start:start + size], preferred_element_type=F32)
        h = (jax.nn.silu(hg) * hu).astype(BF16)
        part = jnp.dot(h, wd_ref[start:start + size, :], preferred_element_type=F32)
        acc = part if acc is None else acc + part
    return acc


def _ffn_dense_kernel(x_ref, wg_ref, wu_ref, wd_ref, g_ref, b_ref, y_ref, *, alpha, ff_chunk):
    x = x_ref[...]
    ffn = _swiglu_tile(x.astype(BF16), wg_ref, wu_ref, wd_ref, ff_chunk)
    y_ref[...] = _layer_norm(alpha * x + ffn, g_ref[...], b_ref[...])


def _ffn_dense(x, wg, wu, wd, ln_g, ln_b, alpha, tm, ff_chunk):
    t, d = x.shape
    assert t % tm == 0
    row = pl.BlockSpec((tm, d), lambda i: (i, 0))
    const = lambda shape: pl.BlockSpec(shape, lambda i: (0, 0), pipeline_mode=pl.Buffered(1))
    return pl.pallas_call(
        functools.partial(_ffn_dense_kernel, alpha=alpha, ff_chunk=ff_chunk),
        grid=(t // tm,),
        in_specs=[row, const(wg.shape), const(wu.shape), const(wd.shape), const((1, d)), const((1, d))],
        out_specs=row,
        out_shape=jax.ShapeDtypeStruct((t, d), F32),
        compiler_params=_params("parallel"),
        name="ffn_dense",
    )(x, wg, wu, wd, ln_g, ln_b)


def _ffn_grouped_kernel(te_ref, nt_ref, x_ref, wg_ref, wu_ref, wd_ref, y_ref, *, ff_chunk):
    @pl.when(pl.program_id(0) < nt_ref[0])
    def _():
        y_ref[...] = _swiglu_tile(x_ref[...], wg_ref, wu_ref, wd_ref, ff_chunk)


def _ffn_grouped(xs, tile_expert, n_tiles_used, wg, wu, wd, tm, ff_chunk):
    r, d = xs.shape
    assert r % tm == 0
    row = pl.BlockSpec((tm, d), lambda i, te, nt: (i, 0))
    expert = lambda shape: pl.BlockSpec((None,) + shape[1:], lambda i, te, nt: (te[i], 0, 0),
                                        pipeline_mode=pl.Buffered(1))
    return pl.pallas_call(
        functools.partial(_ffn_grouped_kernel, ff_chunk=ff_chunk),
        grid_spec=pltpu.PrefetchScalarGridSpec(
            num_scalar_prefetch=2,
            grid=(r // tm,),
            in_specs=[row, expert(wg.shape), expert(wu.shape), expert(wd.shape)],
            out_specs=row,
        ),
        out_shape=jax.ShapeDtypeStruct((r, d), F32),
        compiler_params=_params("arbitrary"),
        name="ffn_grouped",
    )(tile_expert, n_tiles_used, xs, wg, wu, wd)


def _router_kernel(x_ref, wr_ref, e_ref, g_ref):
    logits = lax.dot_general(wr_ref[...], x_ref[...], (((1,), (1,)), ((), ())),
                             precision=lax.Precision.HIGHEST, preferred_element_type=F32)
    idx = lax.broadcasted_iota(jnp.int32, logits.shape, 0)
    m1 = jnp.max(logits, axis=0, keepdims=True)
    i1 = jnp.min(jnp.where(logits == m1, idx, N_EXPERTS), axis=0, keepdims=True)
    rest = jnp.where(idx == i1, -jnp.inf, logits)
    m2 = jnp.max(rest, axis=0, keepdims=True)
    i2 = jnp.min(jnp.where(rest == m2, idx, N_EXPERTS), axis=0, keepdims=True)
    e2 = jnp.exp(m2 - m1)
    denom = 1.0 + e2
    e_ref[...] = jnp.concatenate([i1, i2], axis=0)
    g_ref[...] = jnp.concatenate([1.0 / denom, e2 / denom], axis=0)


def _router(x, w_router_t, tm):
    t, d = x.shape
    assert t % tm == 0
    return pl.pallas_call(
        _router_kernel,
        grid=(t // tm,),
        in_specs=[pl.BlockSpec((tm, d), lambda i: (i, 0)), pl.BlockSpec(w_router_t.shape, lambda i: (0, 0))],
        out_specs=[pl.BlockSpec((2, tm), lambda i: (0, i)), pl.BlockSpec((2, tm), lambda i: (0, i))],
        out_shape=[jax.ShapeDtypeStruct((2, t), jnp.int32), jax.ShapeDtypeStruct((2, t), F32)],
        compiler_params=_params("parallel"),
        name="router",
    )(x, w_router_t)


def _combine_kernel(x_ref, y1_ref, y2_ref, g1_ref, g2_ref, lg_ref, lb_ref, o_ref, *, alpha):
    ffn = g1_ref[...] * y1_ref[...] + g2_ref[...] * y2_ref[...]
    o_ref[...] = _layer_norm(alpha * x_ref[...] + ffn, lg_ref[...], lb_ref[...])


def _combine(x, y1, y2, g1, g2, ln_g, ln_b, alpha, tm):
    t, d = x.shape
    row = pl.BlockSpec((tm, d), lambda i: (i, 0))
    col = pl.BlockSpec((tm, 1), lambda i: (i, 0))
    const = pl.BlockSpec((1, d), lambda i: (0, 0))
    return pl.pallas_call(
        functools.partial(_combine_kernel, alpha=alpha),
        grid=(t // tm,),
        in_specs=[row, row, row, col, col, const, const],
        out_specs=row,
        out_shape=jax.ShapeDtypeStruct((t, d), F32),
        compiler_params=_params("parallel"),
        name="combine",
    )(x, y1, y2, g1, g2, ln_g, ln_b)


def _moe(x, w_router_t, wg, wu, wd, ln_g, ln_b, alpha, tm_route, tm_ffn, ff_chunk):
    t, d = x.shape
    experts, gates = _router(x, w_router_t, tm_route)
    pair_e = experts.reshape(2 * t)
    onehot = (pair_e[:, None] == jnp.arange(N_EXPERTS, dtype=jnp.int32)[None, :]).astype(jnp.int32)
    csum = jnp.cumsum(onehot, axis=0)
    rank = jnp.take_along_axis(csum, pair_e[:, None], axis=1)[:, 0] - 1
    counts = csum[-1]
    tiles_per = (counts + tm_ffn - 1) // tm_ffn
    tile_end = jnp.cumsum(tiles_per)
    starts = (tile_end - tiles_per) * tm_ffn
    pos = starts[pair_e] + rank
    n_tiles = (2 * t) // tm_ffn + N_EXPERTS
    src = jnp.zeros((n_tiles * tm_ffn,), jnp.int32).at[pos].set(jnp.tile(jnp.arange(t, dtype=jnp.int32), 2))
    tile_ids = jnp.arange(n_tiles, dtype=jnp.int32)
    tile_expert = jnp.minimum(jnp.sum((tile_end[None, :] <= tile_ids[:, None]).astype(jnp.int32), axis=1),
                              N_EXPERTS - 1)
    xs = jnp.take(x.astype(BF16), src, axis=0)
    ys = _ffn_grouped(xs, tile_expert, tile_end[-1:].astype(jnp.int32), wg, wu, wd, tm_ffn, ff_chunk)
    y1 = jnp.take(ys, pos[:t], axis=0)
    y2 = jnp.take(ys, pos[t:], axis=0)
    return _combine(x, y1, y2, gates[0][:, None], gates[1][:, None], ln_g, ln_b, alpha, tm_route)


def _spatial_params(w_s, b_s, n_pos):
    reps = CHUNK_MLP // n_pos
    wm = w_s[:, :n_pos, :n_pos] * jnp.tril(jnp.ones((n_pos, n_pos), w_s.dtype))
    eye = jnp.eye(reps, dtype=w_s.dtype)
    wm = jnp.einsum("ab,gij->gaibj", eye, wm).reshape(N_GROUPS, CHUNK_MLP, CHUNK_MLP)
    bias = jnp.tile(b_s[:, :n_pos].T, (reps, 1))
    return wm.astype(BF16), jnp.repeat(bias, GROUP_W, axis=1).astype(F32)


def kernel(x_prompt, x_sample, cache_k, cache_v, w_in, lam_q1, lam_k1, lam_q2, lam_k2, subln_g,
           sgu_ln_g, sgu_ln_b, w_spatial, b_spatial, w_out, ln_mix_g, ln_mix_b,
           w_ffn_gate, w_ffn_up, w_ffn_down, w_router, w_moe_gate, w_moe_up, w_moe_down,
           ln_ffn_g, ln_ffn_b):
    depth = w_in.shape[0]
    batch, seq, d_model = x_prompt.shape
    dec_batch, dec_seq, _ = x_sample.shape
    past_len = cache_k.shape[2]
    alpha = (2.0 * depth) ** 0.25
    t_p, t_s = batch * seq, dec_batch * dec_seq
    n_keys = past_len + dec_seq
    keys_pad = -(-n_keys // LANES) * LANES

    tm_p = min(512, t_p)
    tm_s = min(512, t_s)
    tq = min(512, seq)

    xp = x_prompt.reshape(t_p, d_model)
    xs = x_sample.reshape(t_s, d_model)
    kv_p, kv_s, gv_s = (), (), []
    row2 = lambda a: a.reshape(1, -1).astype(F32)

    for l in range(depth):
        lam0 = _lambda_init(l)
        lam = (jnp.exp(jnp.sum(lam_q1[l].astype(F32) * lam_k1[l].astype(F32)))
               - jnp.exp(jnp.sum(lam_q2[l].astype(F32) * lam_k2[l].astype(F32))) + lam0).reshape(1)
        w_in_l = w_in[l].astype(BF16)
        w_out_l = w_out[l].astype(BF16)
        sub_g = row2(subln_g[l])
        sgu_g, sgu_b = row2(sgu_ln_g[l]), row2(sgu_ln_b[l])
        mix_g, mix_b = row2(ln_mix_g[l]), row2(ln_mix_b[l])
        ffn_g, ffn_b = row2(ln_ffn_g[l]), row2(ln_ffn_b[l])
        wm_p, bs_p = _spatial_params(w_spatial[l], b_spatial[l], CHUNK_MLP)
        wm_s, bs_s = _spatial_params(w_spatial[l], b_spatial[l], dec_seq)

        def channel_mix(x, tm):
            i = l // 2
            if l % 2 == 0:
                return _ffn_dense(x, w_ffn_gate[i].astype(BF16), w_ffn_up[i].astype(BF16),
                                  w_ffn_down[i].astype(BF16), ffn_g, ffn_b, alpha, tm, 512)
            return _moe(x, w_router[i].T.astype(F32), w_moe_gate[i].astype(BF16), w_moe_up[i].astype(BF16),
                        w_moe_down[i].astype(BF16), ffn_g, ffn_b, alpha, tm, min(512, 2 * x.shape[0]), 512)

        q, k, v, kb, vb, u, gvn = _proj(xp, w_in_l, sgu_g, sgu_b, tm_p, l, depth, kv_p)
        kv_p = (k, v)
        shape3 = (batch, seq, -1)
        a = _attn_prompt(q.reshape(shape3), kb.reshape(shape3), vb.reshape(shape3), lam, sub_g, lam0, tq)
        xp = _merge(a.reshape(t_p, -1), u, gvn, xp, wm_p, bs_p, w_out_l, mix_g, mix_b, alpha, tm_p)
        xp = channel_mix(xp, tm_p)

        q, k, v, kb, vb, u, gvn = _proj(xs, w_in_l, sgu_g, sgu_b, tm_s, l, depth, kv_s)
        kv_s = (k, v)
        pad = lambda c, new: jnp.pad(
            jnp.concatenate([c.reshape(dec_batch, past_len, -1).astype(BF16),
                             new.reshape(dec_batch, dec_seq, -1)], axis=1),
            ((0, 0), (0, keys_pad - n_keys), (0, 0)))
        a = _attn_sample(q.reshape(dec_batch, dec_seq, -1), pad(cache_k[l], kb), pad(cache_v[l], vb),
                         n_keys, lam, sub_g, lam0)
        xs = _merge(a.reshape(t_s, -1), u, gvn, xs, wm_s, bs_s, w_out_l, mix_g, mix_b, alpha, tm_s)
        xs = channel_mix(xs, tm_s)
        gv_s.append(gvn.reshape(dec_batch, dec_seq, -1))

    return (xp.reshape(batch, seq, d_model), xs.reshape(dec_batch, dec_seq, d_model),
            kv_p[0].reshape(depth, batch, seq, N_HEADS, 2, D_HEAD),
            kv_p[1].reshape(depth, batch, seq, N_HEADS, HEAD_W),
            kv_s[0].reshape(depth, dec_batch, dec_seq, N_HEADS, 2, D_HEAD),
            kv_s[1].reshape(depth, dec_batch, dec_seq, N_HEADS, HEAD_W),
            jnp.stack(gv_s, axis=0))
```

```python
import functools
import math

import jax
import jax.numpy as jnp
from jax import lax
from jax.experimental import pallas as pl
from jax.experimental.pallas import tpu as pltpu
from jax.experimental.pallas import tpu_sc as plsc

F32 = jnp.float32
BF16 = jnp.bfloat16

N_HEADS = 4
HEAD_W = 128
D_HEAD = 64
N_GROUPS = 4
GROUP_W = 128
CHUNK = 64
CHUNK_MLP = 128
N_EXPERTS = 8
LN_EPS = 1e-5
NEG_INF = -1e30

LANES = 128
MXU_DIM = 256
VMEM_LIMIT_BYTES = 56 * 1024 * 1024


def _lambda_init(layer):
    return 0.8 - 0.6 * math.exp(-0.3 * layer)


def _layer_norm(x, g, b):
    mu = jnp.mean(x, axis=-1, keepdims=True)
    xc = x - mu
    var = jnp.mean(xc * xc, axis=-1, keepdims=True)
    return xc * lax.rsqrt(var + LN_EPS) * g + b


def _params(*semantics):
    return pltpu.CompilerParams(dimension_semantics=semantics, vmem_limit_bytes=VMEM_LIMIT_BYTES)


def _proj_kernel(x_ref, w_ref, g_ref, b_ref, q_ref, k_ref, v_ref, kb_ref, vb_ref, u_ref, gv_ref,
                 *, qk_w, d_a, d_b):
    x = x_ref[...].astype(BF16)

    def cols(start, width):
        return jnp.dot(x, w_ref[:, start:start + width], preferred_element_type=F32)

    q_ref[...] = (cols(0, qk_w) * (D_HEAD ** -0.5)).astype(BF16)
    k = cols(qk_w, qk_w)
    k_ref[...] = k
    kb_ref[...] = k.astype(BF16)
    v = cols(2 * qk_w, d_a)
    v_ref[...] = v
    vb_ref[...] = v.astype(BF16)
    off = 2 * qk_w + d_a
    u_ref[...] = jax.nn.gelu(cols(off, d_b))
    gv = jax.nn.gelu(cols(off + d_b, d_b))
    gv_ref[...] = _layer_norm(gv, g_ref[...], b_ref[...])


def _proj(x, w_in, ln_g, ln_b, tm):
    t, d = x.shape
    qk_w = N_HEADS * HEAD_W
    d_a = N_HEADS * HEAD_W
    d_b = N_GROUPS * GROUP_W
    assert w_in.shape == (d, 2 * qk_w + d_a + 2 * d_b) and t % tm == 0
    row = lambda w: pl.BlockSpec((tm, w), lambda i: (i, 0))
    const = lambda shape: pl.BlockSpec(shape, lambda i: (0, 0))
    return pl.pallas_call(
        functools.partial(_proj_kernel, qk_w=qk_w, d_a=d_a, d_b=d_b),
        grid=(t // tm,),
        in_specs=[row(d), const(w_in.shape), const((1, d_b)), const((1, d_b))],
        out_specs=[row(qk_w), row(qk_w), row(d_a), row(qk_w), row(d_a), row(d_b), row(d_b)],
        out_shape=[
            jax.ShapeDtypeStruct((t, qk_w), BF16),
            jax.ShapeDtypeStruct((t, qk_w), F32),
            jax.ShapeDtypeStruct((t, d_a), F32),
            jax.ShapeDtypeStruct((t, qk_w), BF16),
            jax.ShapeDtypeStruct((t, d_a), BF16),
            jax.ShapeDtypeStruct((t, d_b), F32),
            jax.ShapeDtypeStruct((t, d_b), F32),
        ],
        compiler_params=_params("parallel"),
        name="proj",
    )(x, w_in, ln_g, ln_b)


def _stack_queries(q):
    lane = lax.broadcasted_iota(jnp.int32, q.shape, 1)
    zero = jnp.zeros_like(q)
    return jnp.concatenate([jnp.where(lane < D_HEAD, q, zero), jnp.where(lane >= D_HEAD, q, zero)], axis=0)


def _scores(qs, kb):
    return lax.dot_general(qs, kb, (((1,), (1,)), ((), ())), preferred_element_type=F32)


def _diff_merge(o1, o2, lam, g, lam0):
    o = o1 - lam * o2
    ms = jnp.mean(o * o, axis=-1, keepdims=True)
    return o * lax.rsqrt(ms + LN_EPS) * g * (1.0 - lam0)


def _attn_prompt_kernel(lam_ref, g_ref, q_ref, k_ref, v_ref, o_ref,
                        qs_ref, s_ref, p_ref, a_ref, m_ref, acc_ref, *, tq, lam0):
    i = pl.program_id(2)
    n_rep = tq // LANES
    qs_ref[...] = _stack_queries(q_ref[...])
    m_ref[...] = jnp.full(m_ref.shape, NEG_INF, F32)
    acc_ref[...] = jnp.zeros(acc_ref.shape, F32)
    p_ref[1] = jnp.zeros(p_ref.shape[1:], BF16)
    a_ref[1] = jnp.ones(a_ref.shape[1:], F32)
    ones = jnp.ones((tq, HEAD_W), BF16)

    def scores(j, slot):
        off = pl.multiple_of(j * tq, tq)
        s_ref[slot] = _scores(qs_ref[...], k_ref[pl.ds(off, tq), :])

    def softmax(slot, masked, out_slot=None):
        out_slot = slot if out_slot is None else out_slot
        for r0 in range(0, 2 * tq, CHUNK):
            rows = slice(r0, r0 + CHUNK)
            s = s_ref[slot, rows, :]
            if masked:
                n_vis = (r0 % tq) + CHUNK
                s = jnp.where(lax.broadcasted_iota(jnp.int32, s.shape, 1) < n_vis, s, NEG_INF)
            m_prev = m_ref[rows, :]
            m_new = jnp.maximum(m_prev, jnp.max(s, axis=1, keepdims=True))
            a_ref[out_slot, rows, :] = jnp.exp(m_prev - m_new)
            p_ref[out_slot, rows, :] = jnp.exp(s - jnp.concatenate([m_new] * n_rep, axis=1)).astype(BF16)
            m_ref[rows, :] = m_new

    def pv(j, slot):
        off = pl.multiple_of(j * tq, tq)
        v_ext = jnp.concatenate([v_ref[pl.ds(off, tq), :], ones], axis=1)
        a = a_ref[slot]
        acc_ref[...] = (jnp.concatenate([a, a], axis=1) * acc_ref[...]
                        + jnp.dot(p_ref[slot], v_ext, preferred_element_type=F32))

    def step(t, slot):
        scores(t + 1, 1 - slot)
        pv(jnp.maximum(t - 1, 0), 1 - slot)
        softmax(slot, False)

    def last_step(t, slot, out_slot):
        pv(jnp.maximum(t - 1, 0), 1 - slot)
        softmax(slot, True, out_slot)
        pv(t, out_slot)

    scores(0, 0)

    def pair(u, carry):
        step(2 * u, 0)
        step(2 * u + 1, 1)
        return carry

    lax.fori_loop(0, i // 2, pair, 0)

    @pl.when(i % 2 == 0)
    def _():
        last_step(i, 0, 0)

    @pl.when(i % 2 == 1)
    def _():
        step(i - 1, 0)
        last_step(i, 1, 2)

    acc = acc_ref[...]
    o1 = acc[:tq, :HEAD_W] / acc[:tq, HEAD_W:]
    o2 = acc[tq:, :HEAD_W] / acc[tq:, HEAD_W:]
    o_ref[...] = _diff_merge(o1, o2, lam_ref[0], g_ref[...], lam0).astype(o_ref.dtype)


def _attn_prompt(q, k, v, lam, subln_g, lam0, tq):
    b, s, w = q.shape
    assert s % tq == 0 and tq % CHUNK == 0 and w == N_HEADS * HEAD_W
    q_spec = pl.BlockSpec((None, tq, HEAD_W), lambda bi, h, i: (bi, i, h))
    kv_spec = pl.BlockSpec((None, s, HEAD_W), lambda bi, h, i: (bi, 0, h))
    return pl.pallas_call(
        functools.partial(_attn_prompt_kernel, tq=tq, lam0=lam0),
        grid=(b, N_HEADS, s // tq),
        in_specs=[
            pl.BlockSpec(memory_space=pltpu.SMEM),
            pl.BlockSpec((1, HEAD_W), lambda bi, h, i: (0, 0)),
            q_spec, kv_spec, kv_spec,
        ],
        out_specs=q_spec,
        out_shape=jax.ShapeDtypeStruct((b, s, w), BF16),
        scratch_shapes=[
            pltpu.VMEM((2 * tq, HEAD_W), BF16),
            pltpu.VMEM((2, 2 * tq, tq), F32),
            pltpu.VMEM((3, 2 * tq, tq), BF16),
            pltpu.VMEM((3, 2 * tq, LANES), F32),
            pltpu.VMEM((2 * tq, LANES), F32),
            pltpu.VMEM((2 * tq, 2 * HEAD_W), F32),
        ],
        compiler_params=_params("parallel", "parallel", "arbitrary"),
        name="attn_prompt",
    )(lam, subln_g, q, k, v)


def _attn_sample_kernel(lam_ref, g_ref, q_ref, k_ref, v_ref, o_ref, *, n_q, n_keys, lam0):
    qs = _stack_queries(q_ref[...])
    s = _scores(qs, k_ref[...])
    col = lax.broadcasted_iota(jnp.int32, s.shape, 1)
    s = jnp.where(col < n_keys, s, NEG_INF)
    m = jnp.max(s, axis=1, keepdims=True)
    p = jnp.exp(s - m)
    l = jnp.sum(p, axis=1, keepdims=True)
    o = jnp.dot(p.astype(BF16), v_ref[...], preferred_element_type=F32) / l
    o_ref[...] = _diff_merge(o[:n_q], o[n_q:], lam_ref[0], g_ref[...], lam0).astype(o_ref.dtype)


def _attn_sample(q, k_all, v_all, n_keys, lam, subln_g, lam0):
    b, n_q, w = q.shape
    n_pad = k_all.shape[1]
    q_spec = pl.BlockSpec((None, n_q, HEAD_W), lambda bi, h: (bi, 0, h))
    kv_spec = pl.BlockSpec((None, n_pad, HEAD_W), lambda bi, h: (bi, 0, h))
    return pl.pallas_call(
        functools.partial(_attn_sample_kernel, n_q=n_q, n_keys=n_keys, lam0=lam0),
        grid=(b, N_HEADS),
        in_specs=[
            pl.BlockSpec(memory_space=pltpu.SMEM),
            pl.BlockSpec((1, HEAD_W), lambda bi, h: (0, 0)),
            q_spec, kv_spec, kv_spec,
        ],
        out_specs=q_spec,
        out_shape=jax.ShapeDtypeStruct((b, n_q, w), BF16),
        compiler_params=_params("parallel", "parallel"),
        name="attn_sample",
    )(lam, subln_g, q, k_all, v_all)


def _merge_kernel(a_ref, u_ref, gv_ref, x_ref, wm_ref, bs_ref, wo_ref, g_ref, b_ref, y_ref, gated_ref,
                  *, tm, alpha):
    d_a = a_ref.shape[1]
    for c in range(tm // CHUNK_MLP):
        rows = slice(c * CHUNK_MLP, (c + 1) * CHUNK_MLP)
        for grp in range(N_GROUPS):
            cs = slice(grp * GROUP_W, (grp + 1) * GROUP_W)
            mixed = jnp.dot(wm_ref[grp], gv_ref[rows, cs].astype(BF16), preferred_element_type=F32)
            gated_ref[rows, cs] = (u_ref[rows, cs] * (mixed + bs_ref[:, cs])).astype(BF16)
    mix = jnp.dot(a_ref[...], wo_ref[:d_a, :], preferred_element_type=F32)
    mix = mix + jnp.dot(gated_ref[...], wo_ref[d_a:, :], preferred_element_type=F32)
    y_ref[...] = _layer_norm(alpha * x_ref[...] + mix, g_ref[...], b_ref[...])


def _merge(a, u, gvn, x, wm, bs, w_out, ln_g, ln_b, alpha, tm):
    t, d = x.shape
    d_a, d_b = a.shape[1], u.shape[1]
    assert t % tm == 0 and tm % CHUNK_MLP == 0
    row = lambda w: pl.BlockSpec((tm, w), lambda i: (i, 0))
    const = lambda shape: pl.BlockSpec(shape, lambda i: (0,) * len(shape))
    return pl.pallas_call(
        functools.partial(_merge_kernel, tm=tm, alpha=alpha),
        grid=(t // tm,),
        in_specs=[row(d_a), row(d_b), row(d_b), row(d), const(wm.shape), const(bs.shape),
                  const(w_out.shape), const((1, d)), const((1, d))],
        out_specs=row(d),
        out_shape=jax.ShapeDtypeStruct((t, d), F32),
        scratch_shapes=[pltpu.VMEM((tm, d_b), BF16)],
        compiler_params=_params("parallel"),
        name="merge",
    )(a, u, gvn, x, wm, bs, w_out, ln_g, ln_b)


def _ff_chunks(d_ff, width):
    return [(s, min(width, d_ff - s)) for s in range(0, d_ff, width)]


def _swiglu_tile(x, wg_ref, wu_ref, wd_ref, ff_chunk):
    acc = None
    for start, size in _ff_chunks(wg_ref.shape[1], ff_chunk):
        hg = jnp.dot(x, wg_ref[:, start:start + size], preferred_element_type=F32)
        hu = jnp.dot(x, wu_ref[:, start:start + size], preferred_element_type=F32)
        h = (jax.nn.silu(hg) * hu).astype(BF16)
        part = jnp.dot(h, wd_ref[start:start + size, :], preferred_element_type=F32)
        acc = part if acc is None else acc + part
    return acc


def _ffn_dense_kernel(x_ref, wg_ref, wu_ref, wd_ref, g_ref, b_ref, y_ref, *, alpha, ff_chunk):
    x = x_ref[...]
    ffn = _swiglu_tile(x.astype(BF16), wg_ref, wu_ref, wd_ref, ff_chunk)
    y_ref[...] = _layer_norm(alpha * x + ffn, g_ref[...], b_ref[...])


def _ffn_dense(x, wg, wu, wd, ln_g, ln_b, alpha, tm, ff_chunk):
    t, d = x.shape
    assert t % tm == 0
    row = pl.BlockSpec((tm, d), lambda i: (i, 0))
    const = lambda shape: pl.BlockSpec(shape, lambda i: (0, 0), pipeline_mode=pl.Buffered(1))
    return pl.pallas_call(
        functools.partial(_ffn_dense_kernel, alpha=alpha, ff_chunk=ff_chunk),
        grid=(t // tm,),
        in_specs=[row, const(wg.shape), const(wu.shape), const(wd.shape), const((1, d)), const((1, d))],
        out_specs=row,
        out_shape=jax.ShapeDtypeStruct((t, d), F32),
        compiler_params=_params("parallel"),
        name="ffn_dense",
    )(x, wg, wu, wd, ln_g, ln_b)


def _ffn_grouped_kernel(te_ref, nt_ref, x_ref, wg_ref, wu_ref, wd_ref, y_ref, *, ff_chunk):
    @pl.when(pl.program_id(0) < nt_ref[0])
    def _():
        y_ref[...] = _swiglu_tile(x_ref[...].astype(BF16), wg_ref, wu_ref, wd_ref, ff_chunk)


def _ffn_grouped(xs, tile_expert, n_tiles_used, wg, wu, wd, tm, ff_chunk):
    r, d = xs.shape
    assert r % tm == 0
    row = pl.BlockSpec((tm, d), lambda i, te, nt: (i, 0))
    expert = lambda shape: pl.BlockSpec((None,) + shape[1:], lambda i, te, nt: (te[i], 0, 0),
                                        pipeline_mode=pl.Buffered(1))
    return pl.pallas_call(
        functools.partial(_ffn_grouped_kernel, ff_chunk=ff_chunk),
        grid_spec=pltpu.PrefetchScalarGridSpec(
            num_scalar_prefetch=2,
            grid=(r // tm,),
            in_specs=[row, expert(wg.shape), expert(wu.shape), expert(wd.shape)],
            out_specs=row,
        ),
        out_shape=jax.ShapeDtypeStruct((r, d), F32),
        compiler_params=_params("arbitrary"),
        name="ffn_grouped",
    )(tile_expert, n_tiles_used, xs, wg, wu, wd)


def _router_kernel(x_ref, wr_ref, e_ref, g_ref, r_ref, c_ref):
    logits = lax.dot_general(wr_ref[...], x_ref[...], (((1,), (1,)), ((), ())),
                             precision=lax.Precision.HIGHEST, preferred_element_type=F32)
    tm = logits.shape[1]
    idx = lax.broadcasted_iota(jnp.int32, logits.shape, 0)
    m1 = jnp.max(logits, axis=0, keepdims=True)
    i1 = jnp.min(jnp.where(logits == m1, idx, N_EXPERTS), axis=0, keepdims=True)
    rest = jnp.where(idx == i1, -jnp.inf, logits)
    m2 = jnp.max(rest, axis=0, keepdims=True)
    i2 = jnp.min(jnp.where(rest == m2, idx, N_EXPERTS), axis=0, keepdims=True)
    e2 = jnp.exp(m2 - m1)
    denom = 1.0 + e2
    e_ref[...] = jnp.concatenate([i1, i2], axis=0)
    g_ref[...] = jnp.concatenate([1.0 / denom, e2 / denom], axis=0)
    hit1, hit2 = idx == i1, idx == i2
    hit = jnp.where(jnp.logical_or(hit1, hit2), 1.0, 0.0)
    earlier = jnp.where(lax.broadcasted_iota(jnp.int32, (tm, tm), 0) < lax.broadcasted_iota(jnp.int32, (tm, tm), 1),
                        1.0, 0.0).astype(BF16)
    before = jnp.dot(hit.astype(BF16), earlier, preferred_element_type=F32)
    r_ref[...] = jnp.concatenate([jnp.sum(jnp.where(hit1, before, 0.0), axis=0, keepdims=True),
                                  jnp.sum(jnp.where(hit2, before, 0.0), axis=0, keepdims=True)],
                                 axis=0).astype(jnp.int32)
    c_ref[...] = jnp.broadcast_to(jnp.sum(hit, axis=1, keepdims=True), c_ref.shape)


def _router(x, w_router_t, tm):
    t, d = x.shape
    assert t % tm == 0
    pair = pl.BlockSpec((2, tm), lambda i: (0, i))
    experts, gates, ranks, counts = pl.pallas_call(
        _router_kernel,
        grid=(t // tm,),
        in_specs=[pl.BlockSpec((tm, d), lambda i: (i, 0)), pl.BlockSpec(w_router_t.shape, lambda i: (0, 0))],
        out_specs=[pair, pair, pair, pl.BlockSpec((None, N_EXPERTS, LANES), lambda i: (i, 0, 0))],
        out_shape=[jax.ShapeDtypeStruct((2, t), jnp.int32), jax.ShapeDtypeStruct((2, t), F32),
                   jax.ShapeDtypeStruct((2, t), jnp.int32),
                   jax.ShapeDtypeStruct((t // tm, N_EXPERTS, LANES), F32)],
        compiler_params=_params("parallel"),
        name="router",
    )(x, w_router_t)
    return experts, gates, ranks, counts[:, :, 0].astype(jnp.int32)


SC_WINDOW = 128
SC_COLS = 256


def _sc_mesh():
    return plsc.VectorSubcoreMesh(core_axis_name="core", subcore_axis_name="subcore")


def _sc_pipeline(body, n_rows, in_specs, out_specs):
    return pltpu.emit_pipeline(body, grid=(n_rows // SC_WINDOW,), in_specs=in_specs, out_specs=out_specs,
                               core_axis_name=("core", "subcore"), dimension_semantics=(pltpu.PARALLEL,))


def _sc_dispatch(x, pos, n_rows):
    t, d = x.shape
    assert t % SC_WINDOW == 0 and d % SC_COLS == 0

    @pl.kernel(out_type=jax.ShapeDtypeStruct((n_rows, d), x.dtype), mesh=_sc_mesh())
    def dispatch(x_hbm, pos_hbm, o_hbm):
        for s in range(pos.shape[0]):
            for c in range(d // SC_COLS):
                def body(x_vmem, pos_vmem, c=c):
                    pltpu.sync_copy(x_vmem, o_hbm.at[pos_vmem.at[0], pl.ds(c * SC_COLS, SC_COLS)])

                _sc_pipeline(body, t,
                             [pl.BlockSpec((SC_WINDOW, SC_COLS), lambda i, c=c: (i, c)),
                              pl.BlockSpec((1, SC_WINDOW), lambda i, s=s: (s, i))],
                             [])(x_hbm, pos_hbm)

    return dispatch(x, pos)


def _sc_gather(y, pos):
    n = pos.shape[0]
    d = y.shape[1]
    assert n % SC_WINDOW == 0 and d % SC_COLS == 0

    @pl.kernel(out_type=jax.ShapeDtypeStruct((n, d), y.dtype), mesh=_sc_mesh())
    def gather(y_hbm, pos_hbm, o_hbm):
        for c in range(d // SC_COLS):
            def body(pos_vmem, o_vmem, c=c):
                pltpu.sync_copy(y_hbm.at[pos_vmem.at[0], pl.ds(c * SC_COLS, SC_COLS)], o_vmem)

            _sc_pipeline(body, n,
                         [pl.BlockSpec((1, SC_WINDOW), lambda i: (0, i))],
                         [pl.BlockSpec((SC_WINDOW, SC_COLS), lambda i, c=c: (i, c))])(pos_hbm, o_hbm)

    return gather(y, pos.reshape(1, n))


def _combine_kernel(x_ref, y1_ref, y2_ref, g1_ref, g2_ref, lg_ref, lb_ref, o_ref, *, alpha):
    ffn = g1_ref[...] * y1_ref[...] + g2_ref[...] * y2_ref[...]
    o_ref[...] = _layer_norm(alpha * x_ref[...] + ffn, lg_ref[...], lb_ref[...])


def _combine(x, y_pairs, g1, g2, ln_g, ln_b, alpha, tm):
    t, d = x.shape
    n_blk = t // tm
    row = pl.BlockSpec((tm, d), lambda i: (i, 0))
    row2 = pl.BlockSpec((tm, d), lambda i: (i + n_blk, 0))
    col = pl.BlockSpec((tm, 1), lambda i: (i, 0))
    const = pl.BlockSpec((1, d), lambda i: (0, 0))
    return pl.pallas_call(
        functools.partial(_combine_kernel, alpha=alpha),
        grid=(n_blk,),
        in_specs=[row, row, row2, col, col, const, const],
        out_specs=row,
        out_shape=jax.ShapeDtypeStruct((t, d), F32),
        compiler_params=_params("parallel"),
        name="combine",
    )(x, y_pairs, y_pairs, g1, g2, ln_g, ln_b)


def _moe(x, w_router_t, wg, wu, wd, ln_g, ln_b, alpha, tm_route, tm_ffn, ff_chunk):
    t, d = x.shape
    n_blk = t // tm_route
    experts, gates, ranks, counts = _router(x, w_router_t, tm_route)
    total = jnp.sum(counts, axis=0)
    tiles_per = (total + tm_ffn - 1) // tm_ffn
    tile_end = jnp.cumsum(tiles_per)
    starts = (tile_end - tiles_per) * tm_ffn
    base = starts[None, :] + jnp.cumsum(counts, axis=0) - counts
    experts3 = experts.reshape(2, n_blk, tm_route)
    pos = ranks.reshape(2, n_blk, tm_route)
    for e in range(N_EXPERTS):
        pos = pos + jnp.where(experts3 == e, base[None, :, e, None], 0)
    pos = pos.reshape(2, t)
    n_tiles = (2 * t) // tm_ffn + N_EXPERTS
    tile_ids = jnp.arange(n_tiles, dtype=jnp.int32)
    tile_expert = jnp.minimum(jnp.sum((tile_end[None, :] <= tile_ids[:, None]).astype(jnp.int32), axis=1),
                              N_EXPERTS - 1)
    xs = _sc_dispatch(x, pos, n_tiles * tm_ffn)
    ys = _ffn_grouped(xs, tile_expert, tile_end[-1:].astype(jnp.int32), wg, wu, wd, tm_ffn, ff_chunk)
    y_pairs = _sc_gather(ys, pos.reshape(2 * t))
    return _combine(x, y_pairs, gates[0][:, None], gates[1][:, None], ln_g, ln_b, alpha, tm_route)


def _spatial_params(w_s, b_s, n_pos):
    reps = CHUNK_MLP // n_pos
    wm = w_s[:, :n_pos, :n_pos] * jnp.tril(jnp.ones((n_pos, n_pos), w_s.dtype))
    eye = jnp.eye(reps, dtype=w_s.dtype)
    wm = jnp.einsum("ab,gij->gaibj", eye, wm).reshape(N_GROUPS, CHUNK_MLP, CHUNK_MLP)
    bias = jnp.tile(b_s[:, :n_pos].T, (reps, 1))
    return wm.astype(BF16), jnp.repeat(bias, GROUP_W, axis=1).astype(F32)


def kernel(x_prompt, x_sample, cache_k, cache_v, w_in, lam_q1, lam_k1, lam_q2, lam_k2, subln_g,
           sgu_ln_g, sgu_ln_b, w_spatial, b_spatial, w_out, ln_mix_g, ln_mix_b,
           w_ffn_gate, w_ffn_up, w_ffn_down, w_router, w_moe_gate, w_moe_up, w_moe_down,
           ln_ffn_g, ln_ffn_b):
    depth = w_in.shape[0]
    batch, seq, d_model = x_prompt.shape
    dec_batch, dec_seq, _ = x_sample.shape
    past_len = cache_k.shape[2]
    alpha = (2.0 * depth) ** 0.25
    t_p, t_s = batch * seq, dec_batch * dec_seq
    n_keys = past_len + dec_seq
    keys_pad = -(-n_keys // LANES) * LANES

    tm_p = min(512, t_p)
    tm_s = min(512, t_s)
    tq = min(512, seq)

    xp = x_prompt.reshape(t_p, d_model)
    xs = x_sample.reshape(t_s, d_model)
    k_p, k_s, gv_s, v_p, v_s = [], [], [], [], []
    row2 = lambda a: a.reshape(1, -1).astype(F32)

    for l in range(depth):
        lam0 = _lambda_init(l)
        lam = (jnp.exp(jnp.sum(lam_q1[l].astype(F32) * lam_k1[l].astype(F32)))
               - jnp.exp(jnp.sum(lam_q2[l].astype(F32) * lam_k2[l].astype(F32))) + lam0).reshape(1)
        w_in_l = w_in[l].astype(BF16)
        w_out_l = w_out[l].astype(BF16)
        sub_g = row2(subln_g[l])
        sgu_g, sgu_b = row2(sgu_ln_g[l]), row2(sgu_ln_b[l])
        mix_g, mix_b = row2(ln_mix_g[l]), row2(ln_mix_b[l])
        ffn_g, ffn_b = row2(ln_ffn_g[l]), row2(ln_ffn_b[l])
        wm_p, bs_p = _spatial_params(w_spatial[l], b_spatial[l], CHUNK_MLP)
        wm_s, bs_s = _spatial_params(w_spatial[l], b_spatial[l], dec_seq)

        def channel_mix(x, tm):
            i = l // 2
            if l % 2 == 0:
                return _ffn_dense(x, w_ffn_gate[i].astype(BF16), w_ffn_up[i].astype(BF16),
                                  w_ffn_down[i].astype(BF16), ffn_g, ffn_b, alpha, tm, 512)
            return _moe(x, w_router[i].T.astype(F32), w_moe_gate[i].astype(BF16), w_moe_up[i].astype(BF16),
                        w_moe_down[i].astype(BF16), ffn_g, ffn_b, alpha, tm, min(512, 2 * x.shape[0]), 512)

        q, k, v, kb, vb, u, gvn = _proj(xp, w_in_l, sgu_g, sgu_b, tm_p)
        k_p.append(k)
        v_p.append(v)
        shape3 = (batch, seq, -1)
        a = _attn_prompt(q.reshape(shape3), kb.reshape(shape3), vb.reshape(shape3), lam, sub_g, lam0, tq)
        xp = _merge(a.reshape(t_p, -1), u, gvn, xp, wm_p, bs_p, w_out_l, mix_g, mix_b, alpha, tm_p)
        xp = channel_mix(xp, tm_p)

        q, k, v, kb, vb, u, gvn = _proj(xs, w_in_l, sgu_g, sgu_b, tm_s)
        k_s.append(k)
        v_s.append(v)
        pad = lambda c, new: jnp.pad(
            jnp.concatenate([c.reshape(dec_batch, past_len, -1).astype(BF16),
                             new.reshape(dec_batch, dec_seq, -1)], axis=1),
            ((0, 0), (0, keys_pad - n_keys), (0, 0)))
        a = _attn_sample(q.reshape(dec_batch, dec_seq, -1), pad(cache_k[l], kb), pad(cache_v[l], vb),
                         n_keys, lam, sub_g, lam0)
        xs = _merge(a.reshape(t_s, -1), u, gvn, xs, wm_s, bs_s, w_out_l, mix_g, mix_b, alpha, tm_s)
        xs = channel_mix(xs, tm_s)
        gv_s.append(gvn.reshape(dec_batch, dec_seq, -1))

    return (xp.reshape(batch, seq, d_model), xs.reshape(dec_batch, dec_seq, d_model),
            jnp.stack([k.reshape(batch, seq, N_HEADS, 2, D_HEAD) for k in k_p], axis=0),
            jnp.stack([v.reshape(batch, seq, N_HEADS, HEAD_W) for v in v_p], axis=0),
            jnp.stack([k.reshape(dec_batch, dec_seq, N_HEADS, 2, D_HEAD) for k in k_s], axis=0),
            jnp.stack([v.reshape(dec_batch, dec_seq, N_HEADS, HEAD_W) for v in v_s], axis=0),
            jnp.stack(gv_s, axis=0))
```

```python
import functools
import math

import jax
import jax.numpy as jnp
from jax import lax
from jax.experimental import pallas as pl
from jax.experimental.pallas import tpu as pltpu
from jax.experimental.pallas import tpu_sc as plsc

F32 = jnp.float32
BF16 = jnp.bfloat16

N_HEADS = 4
HEAD_W = 128
D_HEAD = 64
N_GROUPS = 4
GROUP_W = 128
CHUNK = 64
CHUNK_MLP = 128
N_EXPERTS = 8
LN_EPS = 1e-5
NEG_INF = -1e30

LANES = 128
MXU_DIM = 256
VMEM_LIMIT_BYTES = 56 * 1024 * 1024


def _lambda_init(layer):
    return 0.8 - 0.6 * math.exp(-0.3 * layer)


def _layer_norm(x, g, b):
    mu = jnp.mean(x, axis=-1, keepdims=True)
    xc = x - mu
    var = jnp.mean(xc * xc, axis=-1, keepdims=True)
    return xc * lax.rsqrt(var + LN_EPS) * g + b


def _params(*semantics):
    return pltpu.CompilerParams(dimension_semantics=semantics, vmem_limit_bytes=VMEM_LIMIT_BYTES)


def _proj_kernel(x_ref, w_ref, g_ref, b_ref, q_ref, k_ref, v_ref, kb_ref, vb_ref, u_ref, gv_ref,
                 *, qk_w, d_a, d_b):
    x = x_ref[...].astype(BF16)

    def cols(start, width):
        return jnp.dot(x, w_ref[:, start:start + width], preferred_element_type=F32)

    q_ref[...] = (cols(0, qk_w) * (D_HEAD ** -0.5)).astype(BF16)
    k = cols(qk_w, qk_w)
    k_ref[...] = k
    kb_ref[...] = k.astype(BF16)
    v = cols(2 * qk_w, d_a)
    v_ref[...] = v
    vb_ref[...] = v.astype(BF16)
    off = 2 * qk_w + d_a
    u_ref[...] = jax.nn.gelu(cols(off, d_b))
    gv = jax.nn.gelu(cols(off + d_b, d_b))
    gv_ref[...] = _layer_norm(gv, g_ref[...], b_ref[...])


def _proj(x, w_in, ln_g, ln_b, tm):
    t, d = x.shape
    qk_w = N_HEADS * HEAD_W
    d_a = N_HEADS * HEAD_W
    d_b = N_GROUPS * GROUP_W
    assert w_in.shape == (d, 2 * qk_w + d_a + 2 * d_b) and t % tm == 0
    row = lambda w: pl.BlockSpec((tm, w), lambda i: (i, 0))
    const = lambda shape: pl.BlockSpec(shape, lambda i: (0, 0))
    return pl.pallas_call(
        functools.partial(_proj_kernel, qk_w=qk_w, d_a=d_a, d_b=d_b),
        grid=(t // tm,),
        in_specs=[row(d), const(w_in.shape), const((1, d_b)), const((1, d_b))],
        out_specs=[row(qk_w), row(qk_w), row(d_a), row(qk_w), row(d_a), row(d_b), row(d_b)],
        out_shape=[
            jax.ShapeDtypeStruct((t, qk_w), BF16),
            jax.ShapeDtypeStruct((t, qk_w), F32),
            jax.ShapeDtypeStruct((t, d_a), F32),
            jax.ShapeDtypeStruct((t, qk_w), BF16),
            jax.ShapeDtypeStruct((t, d_a), BF16),
            jax.ShapeDtypeStruct((t, d_b), F32),
            jax.ShapeDtypeStruct((t, d_b), F32),
        ],
        compiler_params=_params("parallel"),
        name="proj",
    )(x, w_in, ln_g, ln_b)


def _stack_queries(q):
    lane = lax.broadcasted_iota(jnp.int32, q.shape, 1)
    zero = jnp.zeros_like(q)
    return jnp.concatenate([jnp.where(lane < D_HEAD, q, zero), jnp.where(lane >= D_HEAD, q, zero)], axis=0)


def _scores(qs, kb):
    return lax.dot_general(qs, kb, (((1,), (1,)), ((), ())), preferred_element_type=F32)


def _diff_merge(o1, o2, lam, g, lam0):
    o = o1 - lam * o2
    ms = jnp.mean(o * o, axis=-1, keepdims=True)
    return o * lax.rsqrt(ms + LN_EPS) * g * (1.0 - lam0)


def _attn_prompt_kernel(lam_ref, g_ref, q_ref, k_ref, v_ref, o_ref,
                        qs_ref, s_ref, p_ref, a_ref, m_ref, acc_ref, *, tq, lam0):
    i = pl.program_id(2)
    n_rep = tq // LANES
    qs_ref[...] = _stack_queries(q_ref[...])
    m_ref[...] = jnp.full(m_ref.shape, NEG_INF, F32)
    acc_ref[...] = jnp.zeros(acc_ref.shape, F32)
    p_ref[1] = jnp.zeros(p_ref.shape[1:], BF16)
    a_ref[1] = jnp.ones(a_ref.shape[1:], F32)
    ones = jnp.ones((tq, HEAD_W), BF16)

    def scores(j, slot):
        off = pl.multiple_of(j * tq, tq)
        s_ref[slot] = _scores(qs_ref[...], k_ref[pl.ds(off, tq), :])

    def softmax(slot, masked, out_slot=None):
        out_slot = slot if out_slot is None else out_slot
        for r0 in range(0, 2 * tq, CHUNK):
            rows = slice(r0, r0 + CHUNK)
            s = s_ref[slot, rows, :]
            if masked:
                n_vis = (r0 % tq) + CHUNK
                s = jnp.where(lax.broadcasted_iota(jnp.int32, s.shape, 1) < n_vis, s, NEG_INF)
            m_prev = m_ref[rows, :]
            m_new = jnp.maximum(m_prev, jnp.max(s, axis=1, keepdims=True))
            a_ref[out_slot, rows, :] = jnp.exp(m_prev - m_new)
            p_ref[out_slot, rows, :] = jnp.exp(s - jnp.concatenate([m_new] * n_rep, axis=1)).astype(BF16)
            m_ref[rows, :] = m_new

    def pv(j, slot):
        off = pl.multiple_of(j * tq, tq)
        v_ext = jnp.concatenate([v_ref[pl.ds(off, tq), :], ones], axis=1)
        a = a_ref[slot]
        acc_ref[...] = (jnp.concatenate([a, a], axis=1) * acc_ref[...]
                        + jnp.dot(p_ref[slot], v_ext, preferred_element_type=F32))

    def step(t, slot):
        scores(t + 1, 1 - slot)
        pv(jnp.maximum(t - 1, 0), 1 - slot)
        softmax(slot, False)

    def last_step(t, slot, out_slot):
        pv(jnp.maximum(t - 1, 0), 1 - slot)
        softmax(slot, True, out_slot)
        pv(t, out_slot)

    scores(0, 0)

    def pair(u, carry):
        step(2 * u, 0)
        step(2 * u + 1, 1)
        return carry

    lax.fori_loop(0, i // 2, pair, 0)

    @pl.when(i % 2 == 0)
    def _():
        last_step(i, 0, 0)

    @pl.when(i % 2 == 1)
    def _():
        step(i - 1, 0)
        last_step(i, 1, 2)

    acc = acc_ref[...]
    o1 = acc[:tq, :HEAD_W] / acc[:tq, HEAD_W:]
    o2 = acc[tq:, :HEAD_W] / acc[tq:, HEAD_W:]
    o_ref[...] = _diff_merge(o1, o2, lam_ref[0], g_ref[...], lam0).astype(o_ref.dtype)


def _attn_prompt(q, k, v, lam, subln_g, lam0, tq):
    b, s, w = q.shape
    assert s % tq == 0 and tq % CHUNK == 0 and w == N_HEADS * HEAD_W
    q_spec = pl.BlockSpec((None, tq, HEAD_W), lambda bi, h, i: (bi, i, h))
    kv_spec = pl.BlockSpec((None, s, HEAD_W), lambda bi, h, i: (bi, 0, h))
    return pl.pallas_call(
        functools.partial(_attn_prompt_kernel, tq=tq, lam0=lam0),
        grid=(b, N_HEADS, s // tq),
        in_specs=[
            pl.BlockSpec(memory_space=pltpu.SMEM),
            pl.BlockSpec((1, HEAD_W), lambda bi, h, i: (0, 0)),
            q_spec, kv_spec, kv_spec,
        ],
        out_specs=q_spec,
        out_shape=jax.ShapeDtypeStruct((b, s, w), BF16),
        scratch_shapes=[
            pltpu.VMEM((2 * tq, HEAD_W), BF16),
            pltpu.VMEM((2, 2 * tq, tq), F32),
            pltpu.VMEM((3, 2 * tq, tq), BF16),
            pltpu.VMEM((3, 2 * tq, LANES), F32),
            pltpu.VMEM((2 * tq, LANES), F32),
            pltpu.VMEM((2 * tq, 2 * HEAD_W), F32),
        ],
        compiler_params=_params("parallel", "parallel", "arbitrary"),
        name="attn_prompt",
    )(lam, subln_g, q, k, v)


def _attn_sample_kernel(lam_ref, g_ref, q_ref, k_ref, v_ref, o_ref, *, n_q, n_keys, lam0):
    qs = _stack_queries(q_ref[...])
    s = _scores(qs, k_ref[...])
    col = lax.broadcasted_iota(jnp.int32, s.shape, 1)
    s = jnp.where(col < n_keys, s, NEG_INF)
    m = jnp.max(s, axis=1, keepdims=True)
    p = jnp.exp(s - m)
    l = jnp.sum(p, axis=1, keepdims=True)
    o = jnp.dot(p.astype(BF16), v_ref[...], preferred_element_type=F32) / l
    o_ref[...] = _diff_merge(o[:n_q], o[n_q:], lam_ref[0], g_ref[...], lam0).astype(o_ref.dtype)


def _attn_sample(q, k_all, v_all, n_keys, lam, subln_g, lam0):
    b, n_q, w = q.shape
    n_pad = k_all.shape[1]
    q_spec = pl.BlockSpec((None, n_q, HEAD_W), lambda bi, h: (bi, 0, h))
    kv_spec = pl.BlockSpec((None, n_pad, HEAD_W), lambda bi, h: (bi, 0, h))
    return pl.pallas_call(
        functools.partial(_attn_sample_kernel, n_q=n_q, n_keys=n_keys, lam0=lam0),
        grid=(b, N_HEADS),
        in_specs=[
            pl.BlockSpec(memory_space=pltpu.SMEM),
            pl.BlockSpec((1, HEAD_W), lambda bi, h: (0, 0)),
            q_spec, kv_spec, kv_spec,
        ],
        out_specs=q_spec,
        out_shape=jax.ShapeDtypeStruct((b, n_q, w), BF16),
        compiler_params=_params("parallel", "parallel"),
        name="attn_sample",
    )(lam, subln_g, q, k_all, v_all)


def _merge_kernel(a_ref, u_ref, gv_ref, x_ref, wm_ref, bs_ref, wo_ref, g_ref, b_ref, y_ref, gated_ref,
                  *, tm, alpha):
    d_a = a_ref.shape[1]
    for c in range(tm // CHUNK_MLP):
        rows = slice(c * CHUNK_MLP, (c + 1) * CHUNK_MLP)
        for grp in range(N_GROUPS):
            cs = slice(grp * GROUP_W, (grp + 1) * GROUP_W)
            mixed = jnp.dot(wm_ref[grp], gv_ref[rows, cs].astype(BF16), preferred_element_type=F32)
            gated_ref[rows, cs] = (u_ref[rows, cs] * (mixed + bs_ref[:, cs])).astype(BF16)
    mix = jnp.dot(a_ref[...], wo_ref[:d_a, :], preferred_element_type=F32)
    mix = mix + jnp.dot(gated_ref[...], wo_ref[d_a:, :], preferred_element_type=F32)
    y_ref[...] = _layer_norm(alpha * x_ref[...] + mix, g_ref[...], b_ref[...])


def _merge(a, u, gvn, x, wm, bs, w_out, ln_g, ln_b, alpha, tm):
    t, d = x.shape
    d_a, d_b = a.shape[1], u.shape[1]
    assert t % tm == 0 and tm % CHUNK_MLP == 0
    row = lambda w: pl.BlockSpec((tm, w), lambda i: (i, 0))
    const = lambda shape: pl.BlockSpec(shape, lambda i: (0,) * len(shape))
    return pl.pallas_call(
        functools.partial(_merge_kernel, tm=tm, alpha=alpha),
        grid=(t // tm,),
        in_specs=[row(d_a), row(d_b), row(d_b), row(d), const(wm.shape), const(bs.shape),
                  const(w_out.shape), const((1, d)), const((1, d))],
        out_specs=row(d),
        out_shape=jax.ShapeDtypeStruct((t, d), F32),
        scratch_shapes=[pltpu.VMEM((tm, d_b), BF16)],
        compiler_params=_params("parallel"),
        name="merge",
    )(a, u, gvn, x, wm, bs, w_out, ln_g, ln_b)


def _ff_chunks(d_ff, width):
    return [(s, min(width, d_ff - s)) for s in range(0, d_ff, width)]


def _swiglu_tile(x, wg_ref, wu_ref, wd_ref, ff_chunk):
    acc = None
    for start, size in _ff_chunks(wg_ref.shape[1], ff_chunk):
        hg = jnp.dot(x, wg_ref[:, start:start + size], preferred_element_type=F32)
        hu = jnp.dot(x, wu_ref[:, start:start + size], preferred_element_type=F32)
        h = (jax.nn.silu(hg) * hu).astype(BF16)
        part = jnp.dot(h, wd_ref[start:start + size, :], preferred_element_type=F32)
        acc = part if acc is None else acc + part
    return acc


def _ffn_dense_kernel(x_ref, wg_ref, wu_ref, wd_ref, g_ref, b_ref, y_ref, *, alpha, ff_chunk):
    x = x_ref[...]
    ffn = _swiglu_tile(x.astype(BF16), wg_ref, wu_ref, wd_ref, ff_chunk)
    y_ref[...] = _layer_norm(alpha * x + ffn, g_ref[...], b_ref[...])


def _ffn_dense(x, wg, wu, wd, ln_g, ln_b, alpha, tm, ff_chunk):
    t, d = x.shape
    assert t % tm == 0
    row = pl.BlockSpec((tm, d), lambda i: (i, 0))
    const = lambda shape: pl.BlockSpec(shape, lambda i: (0, 0), pipeline_mode=pl.Buffered(1))
    return pl.pallas_call(
        functools.partial(_ffn_dense_kernel, alpha=alpha, ff_chunk=ff_chunk),
        grid=(t // tm,),
        in_specs=[row, const(wg.shape), const(wu.shape), const(wd.shape), const((1, d)), const((1, d))],
        out_specs=row,
        out_shape=jax.ShapeDtypeStruct((t, d), F32),
        compiler_params=_params("parallel"),
        name="ffn_dense",
    )(x, wg, wu, wd, ln_g, ln_b)


def _ffn_grouped_kernel(te_ref, nt_ref, x_ref, wg_ref, wu_ref, wd_ref, y_ref, *, ff_chunk):
    @pl.when(pl.program_id(0) < nt_ref[0])
    def _():
        y_ref[...] = _swiglu_tile(x_ref[...].astype(BF16), wg_ref, wu_ref, wd_ref, ff_chunk)


def _ffn_grouped(xs, tile_expert, n_tiles_used, wg, wu, wd, tm, ff_chunk):
    r, d = xs.shape
    assert r % tm == 0
    row = pl.BlockSpec((tm, d), lambda i, te, nt: (i, 0))
    expert = lambda shape: pl.BlockSpec((None,) + shape[1:], lambda i, te, nt: (te[i], 0, 0),
                                        pipeline_mode=pl.Buffered(1))
    return pl.pallas_call(
        functools.partial(_ffn_grouped_kernel, ff_chunk=ff_chunk),
        grid_spec=pltpu.PrefetchScalarGridSpec(
            num_scalar_prefetch=2,
            grid=(r // tm,),
            in_specs=[row, expert(wg.shape), expert(wu.shape), expert(wd.shape)],
            out_specs=row,
        ),
        out_shape=jax.ShapeDtypeStruct((r, d), F32),
        compiler_params=_params("arbitrary"),
        name="ffn_grouped",
    )(tile_expert, n_tiles_used, xs, wg, wu, wd)


def _router_kernel(x_ref, wr_ref, e_ref, g_ref, r_ref, c_ref):
    logits = lax.dot_general(wr_ref[...], x_ref[...], (((1,), (1,)), ((), ())),
                             precision=lax.Precision.HIGHEST, preferred_element_type=F32)
    tm = logits.shape[1]
    idx = lax.broadcasted_iota(jnp.int32, logits.shape, 0)
    m1 = jnp.max(logits, axis=0, keepdims=True)
    i1 = jnp.min(jnp.where(logits == m1, idx, N_EXPERTS), axis=0, keepdims=True)
    rest = jnp.where(idx == i1, -jnp.inf, logits)
    m2 = jnp.max(rest, axis=0, keepdims=True)
    i2 = jnp.min(jnp.where(rest == m2, idx, N_EXPERTS), axis=0, keepdims=True)
    e2 = jnp.exp(m2 - m1)
    denom = 1.0 + e2
    e_ref[...] = jnp.concatenate([i1, i2], axis=0)
    g_ref[...] = jnp.concatenate([1.0 / denom, e2 / denom], axis=0)
    hit1, hit2 = idx == i1, idx == i2
    hit = jnp.where(jnp.logical_or(hit1, hit2), 1.0, 0.0)
    earlier = jnp.where(lax.broadcasted_iota(jnp.int32, (tm, tm), 0) < lax.broadcasted_iota(jnp.int32, (tm, tm), 1),
                        1.0, 0.0).astype(BF16)
    before = jnp.dot(hit.astype(BF16), earlier, preferred_element_type=F32)
    r_ref[...] = jnp.concatenate([jnp.sum(jnp.where(hit1, before, 0.0), axis=0, keepdims=True),
                                  jnp.sum(jnp.where(hit2, before, 0.0), axis=0, keepdims=True)],
                                 axis=0).astype(jnp.int32)
    c_ref[...] = jnp.broadcast_to(jnp.sum(hit, axis=1, keepdims=True), c_ref.shape)


def _router(x, w_router_t, tm):
    t, d = x.shape
    assert t % tm == 0
    pair = pl.BlockSpec((2, tm), lambda i: (0, i))
    experts, gates, ranks, counts = pl.pallas_call(
        _router_kernel,
        grid=(t // tm,),
        in_specs=[pl.BlockSpec((tm, d), lambda i: (i, 0)), pl.BlockSpec(w_router_t.shape, lambda i: (0, 0))],
        out_specs=[pair, pair, pair, pl.BlockSpec((None, N_EXPERTS, LANES), lambda i: (i, 0, 0))],
        out_shape=[jax.ShapeDtypeStruct((2, t), jnp.int32), jax.ShapeDtypeStruct((2, t), F32),
                   jax.ShapeDtypeStruct((2, t), jnp.int32),
                   jax.ShapeDtypeStruct((t // tm, N_EXPERTS, LANES), F32)],
        compiler_params=_params("parallel"),
        name="router",
    )(x, w_router_t)
    return experts, gates, ranks, counts[:, :, 0].astype(jnp.int32)


SC_WINDOW = 128
SC_COLS = 256


def _sc_mesh():
    return plsc.VectorSubcoreMesh(core_axis_name="core", subcore_axis_name="subcore")


def _sc_pipeline(body, n_rows, in_specs, out_specs):
    return pltpu.emit_pipeline(body, grid=(n_rows // SC_WINDOW,), in_specs=in_specs, out_specs=out_specs,
                               core_axis_name=("core", "subcore"), dimension_semantics=(pltpu.PARALLEL,))


def _sc_dispatch(x, pos, n_rows):
    t, d = x.shape
    assert t % SC_WINDOW == 0 and d % SC_COLS == 0

    @pl.kernel(out_type=jax.ShapeDtypeStruct((n_rows, d), x.dtype), mesh=_sc_mesh())
    def dispatch(x_hbm, pos_hbm, o_hbm):
        for s in range(pos.shape[0]):
            for c in range(d // SC_COLS):
                def body(x_vmem, pos_vmem, c=c):
                    pltpu.sync_copy(x_vmem, o_hbm.at[pos_vmem.at[0], pl.ds(c * SC_COLS, SC_COLS)])

                _sc_pipeline(body, t,
                             [pl.BlockSpec((SC_WINDOW, SC_COLS), lambda i, c=c: (i, c)),
                              pl.BlockSpec((1, SC_WINDOW), lambda i, s=s: (s, i))],
                             [])(x_hbm, pos_hbm)

    return dispatch(x, pos)


def _sc_gather(y, pos):
    n = pos.shape[0]
    d = y.shape[1]
    assert n % SC_WINDOW == 0 and d % SC_COLS == 0

    @pl.kernel(out_type=jax.ShapeDtypeStruct((n, d), y.dtype), mesh=_sc_mesh())
    def gather(y_hbm, pos_hbm, o_hbm):
        for c in range(d // SC_COLS):
            def body(pos_vmem, o_vmem, c=c):
                pltpu.sync_copy(y_hbm.at[pos_vmem.at[0], pl.ds(c * SC_COLS, SC_COLS)], o_vmem)

            _sc_pipeline(body, n,
                         [pl.BlockSpec((1, SC_WINDOW), lambda i: (0, i))],
                         [pl.BlockSpec((SC_WINDOW, SC_COLS), lambda i, c=c: (i, c))])(pos_hbm, o_hbm)

    return gather(y, pos.reshape(1, n))


def _combine_kernel(x_ref, y1_ref, y2_ref, g1_ref, g2_ref, lg_ref, lb_ref, oa_ref, ob_ref, *, alpha, n_a):
    ffn = g1_ref[...] * y1_ref[...] + g2_ref[...] * y2_ref[...]
    out = _layer_norm(alpha * x_ref[...] + ffn, lg_ref[...], lb_ref[...])

    @pl.when(pl.program_id(0) < n_a)
    def _():
        oa_ref[...] = out

    @pl.when(pl.program_id(0) >= n_a)
    def _():
        ob_ref[...] = out


def _combine(x, y_pairs, g1, g2, ln_g, ln_b, alpha, tm, t_a):
    t, d = x.shape
    n_blk, n_a = t // tm, t_a // tm
    assert t % tm == 0 and t_a % tm == 0 and 0 < n_a < n_blk
    row = pl.BlockSpec((tm, d), lambda i: (i, 0))
    row2 = pl.BlockSpec((tm, d), lambda i: (i + n_blk, 0))
    col = pl.BlockSpec((tm, 1), lambda i: (i, 0))
    const = pl.BlockSpec((1, d), lambda i: (0, 0))
    return pl.pallas_call(
        functools.partial(_combine_kernel, alpha=alpha, n_a=n_a),
        grid=(n_blk,),
        in_specs=[row, row, row2, col, col, const, const],
        out_specs=[pl.BlockSpec((tm, d), lambda i: (jnp.minimum(i, n_a - 1), 0)),
                   pl.BlockSpec((tm, d), lambda i: (jnp.maximum(i - n_a, 0), 0))],
        out_shape=[jax.ShapeDtypeStruct((t_a, d), F32), jax.ShapeDtypeStruct((t - t_a, d), F32)],
        compiler_params=_params("arbitrary"),
        name="combine",
    )(x, y_pairs, y_pairs, g1, g2, ln_g, ln_b)


def _moe(x, t_a, w_router_t, wg, wu, wd, ln_g, ln_b, alpha, tm_route, tm_ffn, ff_chunk):
    t, d = x.shape
    n_blk = t // tm_route
    experts, gates, ranks, counts = _router(x, w_router_t, tm_route)
    total = jnp.sum(counts, axis=0)
    tiles_per = (total + tm_ffn - 1) // tm_ffn
    tile_end = jnp.cumsum(tiles_per)
    starts = (tile_end - tiles_per) * tm_ffn
    base = starts[None, :] + jnp.cumsum(counts, axis=0) - counts
    experts3 = experts.reshape(2, n_blk, tm_route)
    pos = ranks.reshape(2, n_blk, tm_route)
    for e in range(N_EXPERTS):
        pos = pos + jnp.where(experts3 == e, base[None, :, e, None], 0)
    pos = pos.reshape(2, t)
    n_tiles = (2 * t) // tm_ffn + N_EXPERTS
    tile_ids = jnp.arange(n_tiles, dtype=jnp.int32)
    tile_expert = jnp.minimum(jnp.sum((tile_end[None, :] <= tile_ids[:, None]).astype(jnp.int32), axis=1),
                              N_EXPERTS - 1)
    xs = _sc_dispatch(x, pos, n_tiles * tm_ffn)
    ys = _ffn_grouped(xs, tile_expert, tile_end[-1:].astype(jnp.int32), wg, wu, wd, tm_ffn, ff_chunk)
    y_pairs = _sc_gather(ys, pos.reshape(2 * t))
    return _combine(x, y_pairs, gates[0][:, None], gates[1][:, None], ln_g, ln_b, alpha, tm_route, t_a)


def _spatial_params(w_s, b_s, n_pos):
    reps = CHUNK_MLP // n_pos
    wm = w_s[:, :n_pos, :n_pos] * jnp.tril(jnp.ones((n_pos, n_pos), w_s.dtype))
    eye = jnp.eye(reps, dtype=w_s.dtype)
    wm = jnp.einsum("ab,gij->gaibj", eye, wm).reshape(N_GROUPS, CHUNK_MLP, CHUNK_MLP)
    bias = jnp.tile(b_s[:, :n_pos].T, (reps, 1))
    return wm.astype(BF16), jnp.repeat(bias, GROUP_W, axis=1).astype(F32)


def kernel(x_prompt, x_sample, cache_k, cache_v, w_in, lam_q1, lam_k1, lam_q2, lam_k2, subln_g,
           sgu_ln_g, sgu_ln_b, w_spatial, b_spatial, w_out, ln_mix_g, ln_mix_b,
           w_ffn_gate, w_ffn_up, w_ffn_down, w_router, w_moe_gate, w_moe_up, w_moe_down,
           ln_ffn_g, ln_ffn_b):
    depth = w_in.shape[0]
    batch, seq, d_model = x_prompt.shape
    dec_batch, dec_seq, _ = x_sample.shape
    past_len = cache_k.shape[2]
    alpha = (2.0 * depth) ** 0.25
    t_p, t_s = batch * seq, dec_batch * dec_seq
    n_keys = past_len + dec_seq
    keys_pad = -(-n_keys // LANES) * LANES

    tm_p = min(512, t_p)
    tm_s = min(512, t_s)
    tm_moe = math.gcd(tm_p, tm_s)
    tq = min(512, seq)

    xp = x_prompt.reshape(t_p, d_model)
    xs = x_sample.reshape(t_s, d_model)
    k_p, k_s, gv_s, v_p, v_s = [], [], [], [], []
    row2 = lambda a: a.reshape(1, -1).astype(F32)

    for l in range(depth):
        lam0 = _lambda_init(l)
        lam = (jnp.exp(jnp.sum(lam_q1[l].astype(F32) * lam_k1[l].astype(F32)))
               - jnp.exp(jnp.sum(lam_q2[l].astype(F32) * lam_k2[l].astype(F32))) + lam0).reshape(1)
        w_in_l = w_in[l].astype(BF16)
        w_out_l = w_out[l].astype(BF16)
        sub_g = row2(subln_g[l])
        sgu_g, sgu_b = row2(sgu_ln_g[l]), row2(sgu_ln_b[l])
        mix_g, mix_b = row2(ln_mix_g[l]), row2(ln_mix_b[l])
        ffn_g, ffn_b = row2(ln_ffn_g[l]), row2(ln_ffn_b[l])
        wm_p, bs_p = _spatial_params(w_spatial[l], b_spatial[l], CHUNK_MLP)
        wm_s, bs_s = _spatial_params(w_spatial[l], b_spatial[l], dec_seq)

        def channel_mix(x_p, x_s):
            i = l // 2
            if l % 2 == 0:
                dense = lambda x, tm: _ffn_dense(x, w_ffn_gate[i].astype(BF16), w_ffn_up[i].astype(BF16),
                                                 w_ffn_down[i].astype(BF16), ffn_g, ffn_b, alpha, tm, 512)
                return dense(x_p, tm_p), dense(x_s, tm_s)
            return _moe(jnp.concatenate([x_p, x_s], axis=0), t_p, w_router[i].T.astype(F32),
                        w_moe_gate[i].astype(BF16), w_moe_up[i].astype(BF16), w_moe_down[i].astype(BF16),
                        ffn_g, ffn_b, alpha, tm_moe, tm_moe, 512)

        q, k, v, kb, vb, u, gvn = _proj(xp, w_in_l, sgu_g, sgu_b, tm_p)
        k_p.append(k)
        v_p.append(v)
        shape3 = (batch, seq, -1)
        a = _attn_prompt(q.reshape(shape3), kb.reshape(shape3), vb.reshape(shape3), lam, sub_g, lam0, tq)
        xp = _merge(a.reshape(t_p, -1), u, gvn, xp, wm_p, bs_p, w_out_l, mix_g, mix_b, alpha, tm_p)

        q, k, v, kb, vb, u, gvn = _proj(xs, w_in_l, sgu_g, sgu_b, tm_s)
        k_s.append(k)
        v_s.append(v)
        pad = lambda c, new: jnp.pad(
            jnp.concatenate([c.reshape(dec_batch, past_len, -1).astype(BF16),
                             new.reshape(dec_batch, dec_seq, -1)], axis=1),
            ((0, 0), (0, keys_pad - n_keys), (0, 0)))
        a = _attn_sample(q.reshape(dec_batch, dec_seq, -1), pad(cache_k[l], kb), pad(cache_v[l], vb),
                         n_keys, lam, sub_g, lam0)
        xs = _merge(a.reshape(t_s, -1), u, gvn, xs, wm_s, bs_s, w_out_l, mix_g, mix_b, alpha, tm_s)
        gv_s.append(gvn.reshape(dec_batch, dec_seq, -1))

        xp, xs = channel_mix(xp, xs)

    return (xp.reshape(batch, seq, d_model), xs.reshape(dec_batch, dec_seq, d_model),
            jnp.stack([k.reshape(batch, seq, N_HEADS, 2, D_HEAD) for k in k_p], axis=0),
            jnp.stack([v.reshape(batch, seq, N_HEADS, HEAD_W) for v in v_p], axis=0),
            jnp.stack([k.reshape(dec_batch, dec_seq, N_HEADS, 2, D_HEAD) for k in k_s], axis=0),
            jnp.stack([v.reshape(dec_batch, dec_seq, N_HEADS, HEAD_W) for v in v_s], axis=0),
            jnp.stack(gv_s, axis=0))
```

```python
import functools
import math

import jax
import jax.numpy as jnp
from jax import lax
from jax.experimental import pallas as pl
from jax.experimental.pallas import tpu as pltpu
from jax.experimental.pallas import tpu_sc as plsc

F32 = jnp.float32
BF16 = jnp.bfloat16

N_HEADS = 4
HEAD_W = 128
D_HEAD = 64
N_GROUPS = 4
GROUP_W = 128
CHUNK = 64
CHUNK_MLP = 128
N_EXPERTS = 8
LN_EPS = 1e-5
NEG_INF = -1e30

LANES = 128
MXU_DIM = 256
VMEM_LIMIT_BYTES = 56 * 1024 * 1024
CAST_ROWS = 256


def _lambda_init(layer):
    return 0.8 - 0.6 * math.exp(-0.3 * layer)


def _layer_norm(x, g, b):
    mu = jnp.mean(x, axis=-1, keepdims=True)
    xc = x - mu
    var = jnp.mean(xc * xc, axis=-1, keepdims=True)
    return xc * lax.rsqrt(var + LN_EPS) * g + b


def _params(*semantics):
    return pltpu.CompilerParams(dimension_semantics=semantics, vmem_limit_bytes=VMEM_LIMIT_BYTES)


def _cast_kernel(w_ref, o_ref):
    o_ref[...] = w_ref[...].astype(o_ref.dtype)


def _cast_bf16(w, index):
    r, c = w.shape[-2:]
    w4 = w.reshape(w.shape[0], -1, r, c)
    rows = math.gcd(r, CAST_ROWS)
    out = pl.pallas_call(
        _cast_kernel,
        grid=(w4.shape[1], r // rows),
        in_specs=[pl.BlockSpec((None, None, rows, c), lambda e, j: (index, e, j, 0))],
        out_specs=pl.BlockSpec((None, rows, c), lambda e, j: (e, j, 0)),
        out_shape=jax.ShapeDtypeStruct(w4.shape[1:], BF16),
        compiler_params=_params("parallel", "parallel"),
        name="cast_bf16",
    )(w4)
    return out.reshape(w.shape[1:])


def _proj_kernel(x_ref, w_ref, g_ref, b_ref, q_ref, k_ref, v_ref, kb_ref, vb_ref, u_ref, gv_ref,
                 *, qk_w, d_a, d_b):
    x = x_ref[...].astype(BF16)

    def cols(start, width):
        return jnp.dot(x, w_ref[:, start:start + width], preferred_element_type=F32)

    q_ref[...] = (cols(0, qk_w) * (D_HEAD ** -0.5)).astype(BF16)
    k = cols(qk_w, qk_w)
    k_ref[...] = k
    kb_ref[...] = k.astype(BF16)
    v = cols(2 * qk_w, d_a)
    v_ref[...] = v
    vb_ref[...] = v.astype(BF16)
    off = 2 * qk_w + d_a
    u_ref[...] = jax.nn.gelu(cols(off, d_b)).astype(u_ref.dtype)
    gv = jax.nn.gelu(cols(off + d_b, d_b))
    gv_ref[...] = _layer_norm(gv, g_ref[...], b_ref[...]).astype(gv_ref.dtype)


def _proj(x, w_in, ln_g, ln_b, tm, gv_dtype):
    t, d = x.shape
    qk_w = N_HEADS * HEAD_W
    d_a = N_HEADS * HEAD_W
    d_b = N_GROUPS * GROUP_W
    assert w_in.shape == (d, 2 * qk_w + d_a + 2 * d_b) and t % tm == 0
    row = lambda w: pl.BlockSpec((tm, w), lambda i: (i, 0))
    const = lambda shape: pl.BlockSpec(shape, lambda i: (0, 0))
    return pl.pallas_call(
        functools.partial(_proj_kernel, qk_w=qk_w, d_a=d_a, d_b=d_b),
        grid=(t // tm,),
        in_specs=[row(d), const(w_in.shape), const((1, d_b)), const((1, d_b))],
        out_specs=[row(qk_w), row(qk_w), row(d_a), row(qk_w), row(d_a), row(d_b), row(d_b)],
        out_shape=[
            jax.ShapeDtypeStruct((t, qk_w), BF16),
            jax.ShapeDtypeStruct((t, qk_w), F32),
            jax.ShapeDtypeStruct((t, d_a), F32),
            jax.ShapeDtypeStruct((t, qk_w), BF16),
            jax.ShapeDtypeStruct((t, d_a), BF16),
            jax.ShapeDtypeStruct((t, d_b), BF16),
            jax.ShapeDtypeStruct((t, d_b), gv_dtype),
        ],
        compiler_params=_params("parallel"),
        name="proj",
    )(x, w_in, ln_g, ln_b)


def _stack_queries(q):
    lane = lax.broadcasted_iota(jnp.int32, q.shape, 1)
    zero = jnp.zeros_like(q)
    return jnp.concatenate([jnp.where(lane < D_HEAD, q, zero), jnp.where(lane >= D_HEAD, q, zero)], axis=0)


def _scores(qs, kb):
    return lax.dot_general(qs, kb, (((1,), (1,)), ((), ())), preferred_element_type=F32)


def _diff_merge(o1, o2, lam, g, lam0):
    o = o1 - lam * o2
    ms = jnp.mean(o * o, axis=-1, keepdims=True)
    return o * lax.rsqrt(ms + LN_EPS) * g * (1.0 - lam0)


def _attn_prompt_kernel(lam_ref, g_ref, q_ref, k_ref, v_ref, o_ref,
                        qs_ref, s_ref, p_ref, a_ref, m_ref, acc_ref, *, tq, lam0):
    i = pl.program_id(2)
    n_rep = tq // LANES
    qs_ref[...] = _stack_queries(q_ref[...])
    m_ref[...] = jnp.full(m_ref.shape, NEG_INF, F32)
    acc_ref[...] = jnp.zeros(acc_ref.shape, F32)
    p_ref[1] = jnp.zeros(p_ref.shape[1:], BF16)
    a_ref[1] = jnp.ones(a_ref.shape[1:], F32)
    ones = jnp.ones((tq, HEAD_W), BF16)

    def scores(j, slot):
        off = pl.multiple_of(j * tq, tq)
        s_ref[slot] = _scores(qs_ref[...], k_ref[pl.ds(off, tq), :])

    def softmax(slot, masked, out_slot=None):
        out_slot = slot if out_slot is None else out_slot
        for r0 in range(0, 2 * tq, CHUNK):
            rows = slice(r0, r0 + CHUNK)
            s = s_ref[slot, rows, :]
            if masked:
                n_vis = (r0 % tq) + CHUNK
                s = jnp.where(lax.broadcasted_iota(jnp.int32, s.shape, 1) < n_vis, s, NEG_INF)
            m_prev = m_ref[rows, :]
            m_new = jnp.maximum(m_prev, jnp.max(s, axis=1, keepdims=True))
            a_ref[out_slot, rows, :] = jnp.exp(m_prev - m_new)
            p_ref[out_slot, rows, :] = jnp.exp(s - jnp.concatenate([m_new] * n_rep, axis=1)).astype(BF16)
            m_ref[rows, :] = m_new

    def pv(j, slot):
        off = pl.multiple_of(j * tq, tq)
        v_ext = jnp.concatenate([v_ref[pl.ds(off, tq), :], ones], axis=1)
        a = a_ref[slot]
        acc_ref[...] = (jnp.concatenate([a, a], axis=1) * acc_ref[...]
                        + jnp.dot(p_ref[slot], v_ext, preferred_element_type=F32))

    def step(t, slot):
        scores(t + 1, 1 - slot)
        pv(jnp.maximum(t - 1, 0), 1 - slot)
        softmax(slot, False)

    def last_step(t, slot, out_slot):
        pv(jnp.maximum(t - 1, 0), 1 - slot)
        softmax(slot, True, out_slot)
        pv(t, out_slot)

    scores(0, 0)

    def pair(u, carry):
        step(2 * u, 0)
        step(2 * u + 1, 1)
        return carry

    lax.fori_loop(0, i // 2, pair, 0)

    @pl.when(i % 2 == 0)
    def _():
        last_step(i, 0, 0)

    @pl.when(i % 2 == 1)
    def _():
        step(i - 1, 0)
        last_step(i, 1, 2)

    acc = acc_ref[...]
    o1 = acc[:tq, :HEAD_W] / acc[:tq, HEAD_W:]
    o2 = acc[tq:, :HEAD_W] / acc[tq:, HEAD_W:]
    o_ref[...] = _diff_merge(o1, o2, lam_ref[0], g_ref[...], lam0).astype(o_ref.dtype)


def _attn_prompt(q, k, v, lam, subln_g, lam0, tq):
    b, s, w = q.shape
    assert s % tq == 0 and tq % CHUNK == 0 and tq % LANES == 0 and w == N_HEADS * HEAD_W
    q_spec = pl.BlockSpec((None, tq, HEAD_W), lambda bi, h, i: (bi, i, h))
    kv_spec = pl.BlockSpec((None, s, HEAD_W), lambda bi, h, i: (bi, 0, h))
    return pl.pallas_call(
        functools.partial(_attn_prompt_kernel, tq=tq, lam0=lam0),
        grid=(b, N_HEADS, s // tq),
        in_specs=[
            pl.BlockSpec(memory_space=pltpu.SMEM),
            pl.BlockSpec((1, HEAD_W), lambda bi, h, i: (0, 0)),
            q_spec, kv_spec, kv_spec,
        ],
        out_specs=q_spec,
        out_shape=jax.ShapeDtypeStruct((b, s, w), BF16),
        scratch_shapes=[
            pltpu.VMEM((2 * tq, HEAD_W), BF16),
            pltpu.VMEM((2, 2 * tq, tq), F32),
            pltpu.VMEM((3, 2 * tq, tq), BF16),
            pltpu.VMEM((3, 2 * tq, LANES), F32),
            pltpu.VMEM((2 * tq, LANES), F32),
            pltpu.VMEM((2 * tq, 2 * HEAD_W), F32),
        ],
        compiler_params=_params("parallel", "parallel", "arbitrary"),
        name="attn_prompt",
    )(lam, subln_g, q, k, v)


def _attn_sample_kernel(lam_ref, g_ref, q_ref, k_ref, v_ref, o_ref, *, n_q, n_keys, lam0):
    qs = _stack_queries(q_ref[...])
    s = _scores(qs, k_ref[...])
    col = lax.broadcasted_iota(jnp.int32, s.shape, 1)
    s = jnp.where(col < n_keys, s, NEG_INF)
    m = jnp.max(s, axis=1, keepdims=True)
    p = jnp.exp(s - m)
    l = jnp.sum(p, axis=1, keepdims=True)
    o = jnp.dot(p.astype(BF16), v_ref[...], preferred_element_type=F32) / l
    o_ref[...] = _diff_merge(o[:n_q], o[n_q:], lam_ref[0], g_ref[...], lam0).astype(o_ref.dtype)


def _attn_sample(q, k_all, v_all, n_keys, lam, subln_g, lam0):
    b, n_q, w = q.shape
    n_pad = k_all.shape[1]
    q_spec = pl.BlockSpec((None, n_q, HEAD_W), lambda bi, h: (bi, 0, h))
    kv_spec = pl.BlockSpec((None, n_pad, HEAD_W), lambda bi, h: (bi, 0, h))
    return pl.pallas_call(
        functools.partial(_attn_sample_kernel, n_q=n_q, n_keys=n_keys, lam0=lam0),
        grid=(b, N_HEADS),
        in_specs=[
            pl.BlockSpec(memory_space=pltpu.SMEM),
            pl.BlockSpec((1, HEAD_W), lambda bi, h: (0, 0)),
            q_spec, kv_spec, kv_spec,
        ],
        out_specs=q_spec,
        out_shape=jax.ShapeDtypeStruct((b, n_q, w), BF16),
        compiler_params=_params("parallel", "parallel"),
        name="attn_sample",
    )(lam, subln_g, q, k_all, v_all)


def _merge_kernel(a_ref, u_ref, gv_ref, x_ref, wm_ref, bs_ref, wo_ref, g_ref, b_ref, y_ref, gated_ref,
                  *, tm, alpha):
    d_a = a_ref.shape[1]
    for c in range(tm // CHUNK_MLP):
        rows = slice(c * CHUNK_MLP, (c + 1) * CHUNK_MLP)
        for grp in range(N_GROUPS):
            cs = slice(grp * GROUP_W, (grp + 1) * GROUP_W)
            mixed = jnp.dot(wm_ref[grp], gv_ref[rows, cs].astype(BF16), preferred_element_type=F32)
            gated_ref[rows, cs] = (u_ref[rows, cs] * (mixed + bs_ref[:, cs])).astype(BF16)
    mix = jnp.dot(a_ref[...], wo_ref[:d_a, :], preferred_element_type=F32)
    mix = mix + jnp.dot(gated_ref[...], wo_ref[d_a:, :], preferred_element_type=F32)
    y_ref[...] = _layer_norm(alpha * x_ref[...] + mix, g_ref[...], b_ref[...])


def _merge(a, u, gvn, x, wm, bs, w_out, ln_g, ln_b, alpha, tm):
    t, d = x.shape
    d_a, d_b = a.shape[1], u.shape[1]
    assert t % tm == 0 and tm % CHUNK_MLP == 0
    row = lambda w: pl.BlockSpec((tm, w), lambda i: (i, 0))
    const = lambda shape: pl.BlockSpec(shape, lambda i: (0,) * len(shape))
    return pl.pallas_call(
        functools.partial(_merge_kernel, tm=tm, alpha=alpha),
        grid=(t // tm,),
        in_specs=[row(d_a), row(d_b), row(d_b), row(d), const(wm.shape), const(bs.shape),
                  const(w_out.shape), const((1, d)), const((1, d))],
        out_specs=row(d),
        out_shape=jax.ShapeDtypeStruct((t, d), F32),
        scratch_shapes=[pltpu.VMEM((tm, d_b), BF16)],
        compiler_params=_params("parallel"),
        name="merge",
    )(a, u, gvn, x, wm, bs, w_out, ln_g, ln_b)


def _ff_chunks(d_ff, width):
    return [(s, min(width, d_ff - s)) for s in range(0, d_ff, width)]


def _swiglu_tile(x, wg_ref, wu_ref, wd_ref, ff_chunk):
    acc = None
    for start, size in _ff_chunks(wg_ref.shape[1], ff_chunk):
        hg = jnp.dot(x, wg_ref[:, start:start + size], preferred_element_type=F32)
        hu = jnp.dot(x, wu_ref[:, start:start + size], preferred_element_type=F32)
        h = (jax.nn.silu(hg) * hu).astype(BF16)
        part = jnp.dot(h, wd_ref[start:start + size, :], preferred_element_type=F32)
        acc = part if acc is None else acc + part
    return acc


def _ffn_dense_kernel(x_ref, wg_ref, wu_ref, wd_ref, g_ref, b_ref, y_ref, *, alpha, ff_chunk):
    x = x_ref[...]
    ffn = _swiglu_tile(x.astype(BF16), wg_ref, wu_ref, wd_ref, ff_chunk)
    y_ref[...] = _layer_norm(alpha * x + ffn, g_ref[...], b_ref[...])


def _ffn_dense(x, wg, wu, wd, ln_g, ln_b, alpha, tm, ff_chunk):
    t, d = x.shape
    assert t % tm == 0
    row = pl.BlockSpec((tm, d), lambda i: (i, 0))
    const = lambda shape: pl.BlockSpec(shape, lambda i: (0, 0), pipeline_mode=pl.Buffered(1))
    return pl.pallas_call(
        functools.partial(_ffn_dense_kernel, alpha=alpha, ff_chunk=ff_chunk),
        grid=(t // tm,),
        in_specs=[row, const(wg.shape), const(wu.shape), const(wd.shape), const((1, d)), const((1, d))],
        out_specs=row,
        out_shape=jax.ShapeDtypeStruct((t, d), F32),
        compiler_params=_params("parallel"),
        name="ffn_dense",
    )(x, wg, wu, wd, ln_g, ln_b)


def _ffn_grouped_kernel(te_ref, nt_ref, x_ref, wg_ref, wu_ref, wd_ref, y_ref, *, ff_chunk):
    @pl.when(pl.program_id(0) < nt_ref[0])
    def _():
        y_ref[...] = _swiglu_tile(x_ref[...].astype(BF16), wg_ref, wu_ref, wd_ref, ff_chunk)


def _ffn_grouped(xs, tile_expert, n_tiles_used, wg, wu, wd, tm, ff_chunk):
    r, d = xs.shape
    assert r % tm == 0
    row = pl.BlockSpec((tm, d), lambda i, te, nt: (i, 0))
    expert = lambda shape: pl.BlockSpec((None,) + shape[1:], lambda i, te, nt: (te[i], 0, 0),
                                        pipeline_mode=pl.Buffered(1))
    return pl.pallas_call(
        functools.partial(_ffn_grouped_kernel, ff_chunk=ff_chunk),
        grid_spec=pltpu.PrefetchScalarGridSpec(
            num_scalar_prefetch=2,
            grid=(r // tm,),
            in_specs=[row, expert(wg.shape), expert(wu.shape), expert(wd.shape)],
            out_specs=row,
        ),
        out_shape=jax.ShapeDtypeStruct((r, d), F32),
        compiler_params=_params("arbitrary"),
        name="ffn_grouped",
    )(tile_expert, n_tiles_used, xs, wg, wu, wd)


def _router_kernel(x_ref, wr_ref, e_ref, g_ref, r_ref, c_ref):
    logits = lax.dot_general(wr_ref[...], x_ref[...], (((1,), (1,)), ((), ())),
                             precision=lax.Precision.HIGHEST, preferred_element_type=F32)
    tm = logits.shape[1]
    idx = lax.broadcasted_iota(jnp.int32, logits.shape, 0)
    m1 = jnp.max(logits, axis=0, keepdims=True)
    i1 = jnp.min(jnp.where(logits == m1, idx, N_EXPERTS), axis=0, keepdims=True)
    rest = jnp.where(idx == i1, -jnp.inf, logits)
    m2 = jnp.max(rest, axis=0, keepdims=True)
    i2 = jnp.min(jnp.where(rest == m2, idx, N_EXPERTS), axis=0, keepdims=True)
    e2 = jnp.exp(m2 - m1)
    denom = 1.0 + e2
    e_ref[...] = jnp.concatenate([i1, i2], axis=0)
    g_ref[...] = jnp.concatenate([1.0 / denom, e2 / denom], axis=0)
    hit1, hit2 = idx == i1, idx == i2
    hit = jnp.where(jnp.logical_or(hit1, hit2), 1.0, 0.0)
    earlier = jnp.where(lax.broadcasted_iota(jnp.int32, (tm, tm), 0) < lax.broadcasted_iota(jnp.int32, (tm, tm), 1),
                        1.0, 0.0).astype(BF16)
    before = jnp.dot(hit.astype(BF16), earlier, preferred_element_type=F32)
    r_ref[...] = jnp.concatenate([jnp.sum(jnp.where(hit1, before, 0.0), axis=0, keepdims=True),
                                  jnp.sum(jnp.where(hit2, before, 0.0), axis=0, keepdims=True)],
                                 axis=0).astype(jnp.int32)
    c_ref[...] = jnp.broadcast_to(jnp.sum(hit, axis=1, keepdims=True), c_ref.shape)


def _router(x, w_router_t, tm):
    t, d = x.shape
    assert t % tm == 0
    pair = pl.BlockSpec((2, tm), lambda i: (0, i))
    experts, gates, ranks, counts = pl.pallas_call(
        _router_kernel,
        grid=(t // tm,),
        in_specs=[pl.BlockSpec((tm, d), lambda i: (i, 0)), pl.BlockSpec(w_router_t.shape, lambda i: (0, 0))],
        out_specs=[pair, pair, pair, pl.BlockSpec((None, N_EXPERTS, LANES), lambda i: (i, 0, 0))],
        out_shape=[jax.ShapeDtypeStruct((2, t), jnp.int32), jax.ShapeDtypeStruct((2, t), F32),
                   jax.ShapeDtypeStruct((2, t), jnp.int32),
                   jax.ShapeDtypeStruct((t // tm, N_EXPERTS, LANES), F32)],
        compiler_params=_params("parallel"),
        name="router",
    )(x, w_router_t)
    return experts, gates, ranks, counts[:, :, 0].astype(jnp.int32)


SC_WINDOW = 128
SC_COLS = 256


def _sc_mesh():
    return plsc.VectorSubcoreMesh(core_axis_name="core", subcore_axis_name="subcore")


def _sc_pipeline(body, n_rows, in_specs, out_specs):
    return pltpu.emit_pipeline(body, grid=(n_rows // SC_WINDOW,), in_specs=in_specs, out_specs=out_specs,
                               core_axis_name=("core", "subcore"), dimension_semantics=(pltpu.PARALLEL,))


def _sc_dispatch(x, pos, n_rows):
    t, d = x.shape
    assert t % SC_WINDOW == 0 and d % SC_COLS == 0

    @pl.kernel(out_type=jax.ShapeDtypeStruct((n_rows, d), x.dtype), mesh=_sc_mesh())
    def dispatch(x_hbm, pos_hbm, o_hbm):
        for s in range(pos.shape[0]):
            for c in range(d // SC_COLS):
                def body(x_vmem, pos_vmem, c=c):
                    pltpu.sync_copy(x_vmem, o_hbm.at[pos_vmem.at[0], pl.ds(c * SC_COLS, SC_COLS)])

                _sc_pipeline(body, t,
                             [pl.BlockSpec((SC_WINDOW, SC_COLS), lambda i, c=c: (i, c)),
                              pl.BlockSpec((1, SC_WINDOW), lambda i, s=s: (s, i))],
                             [])(x_hbm, pos_hbm)

    return dispatch(x, pos)


def _sc_gather(y, pos):
    n = pos.shape[0]
    d = y.shape[1]
    assert n % SC_WINDOW == 0 and d % SC_COLS == 0

    @pl.kernel(out_type=jax.ShapeDtypeStruct((n, d), y.dtype), mesh=_sc_mesh())
    def gather(y_hbm, pos_hbm, o_hbm):
        for c in range(d // SC_COLS):
            def body(pos_vmem, o_vmem, c=c):
                pltpu.sync_copy(y_hbm.at[pos_vmem.at[0], pl.ds(c * SC_COLS, SC_COLS)], o_vmem)

            _sc_pipeline(body, n,
                         [pl.BlockSpec((1, SC_WINDOW), lambda i: (0, i))],
                         [pl.BlockSpec((SC_WINDOW, SC_COLS), lambda i, c=c: (i, c))])(pos_hbm, o_hbm)

    return gather(y, pos.reshape(1, n))


def _combine_kernel(x_ref, y1_ref, y2_ref, g1_ref, g2_ref, lg_ref, lb_ref, oa_ref, ob_ref, *, alpha, n_a):
    ffn = g1_ref[...] * y1_ref[...] + g2_ref[...] * y2_ref[...]
    out = _layer_norm(alpha * x_ref[...] + ffn, lg_ref[...], lb_ref[...])

    @pl.when(pl.program_id(0) < n_a)
    def _():
        oa_ref[...] = out

    @pl.when(pl.program_id(0) >= n_a)
    def _():
        ob_ref[...] = out


def _combine(x, y_pairs, g1, g2, ln_g, ln_b, alpha, tm, t_a):
    t, d = x.shape
    n_blk, n_a = t // tm, t_a // tm
    assert t % tm == 0 and t_a % tm == 0 and 0 < n_a < n_blk
    row = pl.BlockSpec((tm, d), lambda i: (i, 0))
    row2 = pl.BlockSpec((tm, d), lambda i: (i + n_blk, 0))
    col = pl.BlockSpec((tm, 1), lambda i: (i, 0))
    const = pl.BlockSpec((1, d), lambda i: (0, 0))
    return pl.pallas_call(
        functools.partial(_combine_kernel, alpha=alpha, n_a=n_a),
        grid=(n_blk,),
        in_specs=[row, row, row2, col, col, const, const],
        out_specs=[pl.BlockSpec((tm, d), lambda i: (jnp.minimum(i, n_a - 1), 0)),
                   pl.BlockSpec((tm, d), lambda i: (jnp.maximum(i - n_a, 0), 0))],
        out_shape=[jax.ShapeDtypeStruct((t_a, d), F32), jax.ShapeDtypeStruct((t - t_a, d), F32)],
        compiler_params=_params("arbitrary"),
        name="combine",
    )(x, y_pairs, y_pairs, g1, g2, ln_g, ln_b)


def _moe(x, t_a, w_router_t, wg, wu, wd, ln_g, ln_b, alpha, tm_route, tm_ffn, ff_chunk):
    t, d = x.shape
    n_blk = t // tm_route
    experts, gates, ranks, counts = _router(x, w_router_t, tm_route)
    total = jnp.sum(counts, axis=0)
    tiles_per = (total + tm_ffn - 1) // tm_ffn
    tile_end = jnp.cumsum(tiles_per)
    starts = (tile_end - tiles_per) * tm_ffn
    base = starts[None, :] + jnp.cumsum(counts, axis=0) - counts
    experts3 = experts.reshape(2, n_blk, tm_route)
    pos = ranks.reshape(2, n_blk, tm_route)
    for e in range(N_EXPERTS):
        pos = pos + jnp.where(experts3 == e, base[None, :, e, None], 0)
    pos = pos.reshape(2, t)
    n_tiles = (2 * t) // tm_ffn + N_EXPERTS
    tile_ids = jnp.arange(n_tiles, dtype=jnp.int32)
    tile_expert = jnp.minimum(jnp.sum((tile_end[None, :] <= tile_ids[:, None]).astype(jnp.int32), axis=1),
                              N_EXPERTS - 1)
    xs = _sc_dispatch(x, pos, n_tiles * tm_ffn)
    ys = _ffn_grouped(xs, tile_expert, tile_end[-1:].astype(jnp.int32), wg, wu, wd, tm_ffn, ff_chunk)
    y_pairs = _sc_gather(ys, pos.reshape(2 * t))
    return _combine(x, y_pairs, gates[0][:, None], gates[1][:, None], ln_g, ln_b, alpha, tm_route, t_a)


def _spatial_params(w_s, b_s, n_pos):
    reps = CHUNK_MLP // n_pos
    wm = w_s[:, :n_pos, :n_pos] * jnp.tril(jnp.ones((n_pos, n_pos), w_s.dtype))
    eye = jnp.eye(reps, dtype=w_s.dtype)
    wm = jnp.einsum("ab,gij->gaibj", eye, wm).reshape(N_GROUPS, CHUNK_MLP, CHUNK_MLP)
    bias = jnp.tile(b_s[:, :n_pos].T, (reps, 1))
    return wm.astype(BF16), jnp.repeat(bias, GROUP_W, axis=1).astype(F32)


def kernel(x_prompt, x_sample, cache_k, cache_v, w_in, lam_q1, lam_k1, lam_q2, lam_k2, subln_g,
           sgu_ln_g, sgu_ln_b, w_spatial, b_spatial, w_out, ln_mix_g, ln_mix_b,
           w_ffn_gate, w_ffn_up, w_ffn_down, w_router, w_moe_gate, w_moe_up, w_moe_down,
           ln_ffn_g, ln_ffn_b):
    depth = w_in.shape[0]
    batch, seq, d_model = x_prompt.shape
    dec_batch, dec_seq, _ = x_sample.shape
    past_len = cache_k.shape[2]
    alpha = (2.0 * depth) ** 0.25
    t_p, t_s = batch * seq, dec_batch * dec_seq
    n_keys = past_len + dec_seq
    keys_pad = -(-n_keys // LANES) * LANES

    tm_p = min(512, t_p)
    tm_s = min(512, t_s)
    tm_moe = math.gcd(tm_p, tm_s)
    tq = min(512, seq)

    xp = x_prompt.reshape(t_p, d_model)
    xs = x_sample.reshape(t_s, d_model)
    k_p, k_s, gv_s, v_p, v_s = [], [], [], [], []
    row2 = lambda a: a.reshape(1, -1).astype(F32)

    for l in range(depth):
        lam0 = _lambda_init(l)
        lam = (jnp.exp(jnp.sum(lam_q1[l].astype(F32) * lam_k1[l].astype(F32)))
               - jnp.exp(jnp.sum(lam_q2[l].astype(F32) * lam_k2[l].astype(F32))) + lam0).reshape(1)
        w_in_l = _cast_bf16(w_in, l)
        w_out_l = _cast_bf16(w_out, l)
        sub_g = row2(subln_g[l])
        sgu_g, sgu_b = row2(sgu_ln_g[l]), row2(sgu_ln_b[l])
        mix_g, mix_b = row2(ln_mix_g[l]), row2(ln_mix_b[l])
        ffn_g, ffn_b = row2(ln_ffn_g[l]), row2(ln_ffn_b[l])
        wm_p, bs_p = _spatial_params(w_spatial[l], b_spatial[l], CHUNK_MLP)
        wm_s, bs_s = _spatial_params(w_spatial[l], b_spatial[l], dec_seq)

        def channel_mix(x_p, x_s):
            i = l // 2
            if l % 2 == 0:
                wg, wu, wd = (_cast_bf16(w, i) for w in (w_ffn_gate, w_ffn_up, w_ffn_down))
                dense = lambda x, tm: _ffn_dense(x, wg, wu, wd, ffn_g, ffn_b, alpha, tm, 512)
                return dense(x_p, tm_p), dense(x_s, tm_s)
            wg, wu, wd = (_cast_bf16(w, i) for w in (w_moe_gate, w_moe_up, w_moe_down))
            return _moe(jnp.concatenate([x_p, x_s], axis=0), t_p, w_router[i].T.astype(F32), wg, wu, wd,
                        ffn_g, ffn_b, alpha, tm_moe, tm_moe, 512)

        q, k, v, kb, vb, u, gvn = _proj(xp, w_in_l, sgu_g, sgu_b, tm_p, BF16)
        k_p.append(k)
        v_p.append(v)
        shape3 = (batch, seq, -1)
        a = _attn_prompt(q.reshape(shape3), kb.reshape(shape3), vb.reshape(shape3), lam, sub_g, lam0, tq)
        xp = _merge(a.reshape(t_p, -1), u, gvn, xp, wm_p, bs_p, w_out_l, mix_g, mix_b, alpha, tm_p)

        q, k, v, kb, vb, u, gvn = _proj(xs, w_in_l, sgu_g, sgu_b, tm_s, F32)
        k_s.append(k)
        v_s.append(v)
        pad = lambda c, new: jnp.pad(
            jnp.concatenate([c.reshape(dec_batch, past_len, -1).astype(BF16),
                             new.reshape(dec_batch, dec_seq, -1)], axis=1),
            ((0, 0), (0, keys_pad - n_keys), (0, 0)))
        a = _attn_sample(q.reshape(dec_batch, dec_seq, -1), pad(cache_k[l], kb), pad(cache_v[l], vb),
                         n_keys, lam, sub_g, lam0)
        xs = _merge(a.reshape(t_s, -1), u, gvn, xs, wm_s, bs_s, w_out_l, mix_g, mix_b, alpha, tm_s)
        gv_s.append(gvn.reshape(dec_batch, dec_seq, -1))

        xp, xs = channel_mix(xp, xs)

    return (xp.reshape(batch, seq, d_model), xs.reshape(dec_batch, dec_seq, d_model),
            jnp.stack([k.reshape(batch, seq, N_HEADS, 2, D_HEAD) for k in k_p], axis=0),
            jnp.stack([v.reshape(batch, seq, N_HEADS, HEAD_W) for v in v_p], axis=0),
            jnp.stack([k.reshape(dec_batch, dec_seq, N_HEADS, 2, D_HEAD) for k in k_s], axis=0),
            jnp.stack([v.reshape(dec_batch, dec_seq, N_HEADS, HEAD_W) for v in v_s], axis=0),
            jnp.stack(gv_s, axis=0))
```

```python
import functools
import math

import jax
import jax.numpy as jnp
from jax import lax
from jax.experimental import pallas as pl
from jax.experimental.pallas import tpu as pltpu
from jax.experimental.pallas import tpu_sc as plsc

F32 = jnp.float32
BF16 = jnp.bfloat16

N_HEADS = 4
HEAD_W = 128
D_HEAD = 64
N_GROUPS = 4
GROUP_W = 128
CHUNK = 64
CHUNK_MLP = 128
N_EXPERTS = 8
LN_EPS = 1e-5
NEG_INF = -1e30

LANES = 128
MXU_DIM = 256
VMEM_LIMIT_BYTES = 56 * 1024 * 1024
CAST_ROWS = 256
UNROLL = 8


def _lambda_init(layer):
    return 0.8 - 0.6 * math.exp(-0.3 * layer)


def _layer_norm(x, g, b):
    mu = jnp.mean(x, axis=-1, keepdims=True)
    xc = x - mu
    var = jnp.mean(xc * xc, axis=-1, keepdims=True)
    return xc * lax.rsqrt(var + LN_EPS) * g + b


def _params(*semantics):
    return pltpu.CompilerParams(dimension_semantics=semantics, vmem_limit_bytes=VMEM_LIMIT_BYTES)


def _cast_kernel(w_ref, o_ref):
    o_ref[...] = w_ref[...].astype(o_ref.dtype)


def _cast_bf16(w, index):
    r, c = w.shape[-2:]
    w4 = w.reshape(w.shape[0], -1, r, c)
    rows = math.gcd(r, CAST_ROWS)
    out = pl.pallas_call(
        _cast_kernel,
        grid=(w4.shape[1], r // rows),
        in_specs=[pl.BlockSpec((None, None, rows, c), lambda e, j: (index, e, j, 0))],
        out_specs=pl.BlockSpec((None, rows, c), lambda e, j: (e, j, 0)),
        out_shape=jax.ShapeDtypeStruct(w4.shape[1:], BF16),
        compiler_params=_params("parallel", "parallel"),
        name="cast_bf16",
    )(w4)
    return out.reshape(w.shape[1:])


def _proj_kernel(x_ref, w_ref, g_ref, b_ref, q_ref, k_ref, v_ref, kb_ref, vb_ref, u_ref, gv_ref,
                 *, qk_w, d_a, d_b):
    x = x_ref[...].astype(BF16)

    def cols(start, width):
        return jnp.dot(x, w_ref[:, start:start + width], preferred_element_type=F32)

    q_ref[...] = (cols(0, qk_w) * (D_HEAD ** -0.5)).astype(BF16)
    k = cols(qk_w, qk_w)
    k_ref[...] = k
    kb_ref[...] = k.astype(BF16)
    v = cols(2 * qk_w, d_a)
    v_ref[...] = v
    vb_ref[...] = v.astype(BF16)
    off = 2 * qk_w + d_a
    u_ref[...] = jax.nn.gelu(cols(off, d_b)).astype(u_ref.dtype)
    gv = jax.nn.gelu(cols(off + d_b, d_b))
    gv_ref[...] = _layer_norm(gv, g_ref[...], b_ref[...]).astype(gv_ref.dtype)


def _proj(x, w_in, ln_g, ln_b, tm, gv_dtype):
    t, d = x.shape
    qk_w = N_HEADS * HEAD_W
    d_a = N_HEADS * HEAD_W
    d_b = N_GROUPS * GROUP_W
    assert w_in.shape == (d, 2 * qk_w + d_a + 2 * d_b) and t % tm == 0
    row = lambda w: pl.BlockSpec((tm, w), lambda i: (i, 0))
    const = lambda shape: pl.BlockSpec(shape, lambda i: (0, 0))
    return pl.pallas_call(
        functools.partial(_proj_kernel, qk_w=qk_w, d_a=d_a, d_b=d_b),
        grid=(t // tm,),
        in_specs=[row(d), const(w_in.shape), const((1, d_b)), const((1, d_b))],
        out_specs=[row(qk_w), row(qk_w), row(d_a), row(qk_w), row(d_a), row(d_b), row(d_b)],
        out_shape=[
            jax.ShapeDtypeStruct((t, qk_w), BF16),
            jax.ShapeDtypeStruct((t, qk_w), F32),
            jax.ShapeDtypeStruct((t, d_a), F32),
            jax.ShapeDtypeStruct((t, qk_w), BF16),
            jax.ShapeDtypeStruct((t, d_a), BF16),
            jax.ShapeDtypeStruct((t, d_b), BF16),
            jax.ShapeDtypeStruct((t, d_b), gv_dtype),
        ],
        compiler_params=_params("parallel"),
        name="proj",
    )(x, w_in, ln_g, ln_b)


def _stack_queries(q):
    lane = lax.broadcasted_iota(jnp.int32, q.shape, 1)
    zero = jnp.zeros_like(q)
    return jnp.concatenate([jnp.where(lane < D_HEAD, q, zero), jnp.where(lane >= D_HEAD, q, zero)], axis=0)


def _scores(qs, kb):
    return lax.dot_general(qs, kb, (((1,), (1,)), ((), ())), preferred_element_type=F32)


def _diff_merge(o1, o2, lam, g, lam0):
    o = o1 - lam * o2
    ms = jnp.mean(o * o, axis=-1, keepdims=True)
    return o * lax.rsqrt(ms + LN_EPS) * g * (1.0 - lam0)


def _attn_prompt_kernel(lam_ref, g_ref, q_ref, k_ref, v_ref, o_ref,
                        qs_ref, s_ref, p_ref, a_ref, m_ref, acc_ref, *, tq, lam0):
    i = pl.program_id(2)
    n_rep = tq // LANES
    qs_ref[...] = _stack_queries(q_ref[...])
    m_ref[...] = jnp.full(m_ref.shape, NEG_INF, F32)
    acc_ref[...] = jnp.zeros(acc_ref.shape, F32)
    p_ref[UNROLL - 1] = jnp.zeros(p_ref.shape[1:], BF16)
    a_ref[UNROLL - 1] = jnp.ones(a_ref.shape[1:], F32)
    ones = jnp.ones((tq, HEAD_W), BF16)

    def scores(j, slot):
        off = pl.multiple_of(j * tq, tq)
        s_ref[slot] = _scores(qs_ref[...], k_ref[pl.ds(off, tq), :])

    def softmax(slot, masked, out_slot=None):
        out_slot = slot if out_slot is None else out_slot
        for r0 in range(0, 2 * tq, CHUNK):
            rows = slice(r0, r0 + CHUNK)
            s = s_ref[slot, rows, :]
            if masked:
                n_vis = (r0 % tq) + CHUNK
                s = jnp.where(lax.broadcasted_iota(jnp.int32, s.shape, 1) < n_vis, s, NEG_INF)
            m_prev = m_ref[rows, :]
            m_new = jnp.maximum(m_prev, jnp.max(s, axis=1, keepdims=True))
            a_ref[out_slot, rows, :] = jnp.exp(m_prev - m_new)
            p_ref[out_slot, rows, :] = jnp.exp(s - jnp.concatenate([m_new] * n_rep, axis=1)).astype(BF16)
            m_ref[rows, :] = m_new

    def pv(j, slot):
        off = pl.multiple_of(j * tq, tq)
        v_ext = jnp.concatenate([v_ref[pl.ds(off, tq), :], ones], axis=1)
        a = a_ref[slot]
        acc_ref[...] = (jnp.concatenate([a, a], axis=1) * acc_ref[...]
                        + jnp.dot(p_ref[slot], v_ext, preferred_element_type=F32))

    def step(t, slot):
        nxt, prev = (slot + 1) % UNROLL, (slot - 1) % UNROLL
        scores(t + 1, nxt)
        pv(jnp.maximum(t - 1, 0), prev)
        softmax(slot, False)

    def last_step(t, slot, out_slot):
        pv(jnp.maximum(t - 1, 0), (slot - 1) % UNROLL)
        softmax(slot, True, out_slot)
        pv(t, out_slot)

    scores(0, 0)

    def group(w, carry):
        for k in range(UNROLL):
            step(UNROLL * w + k, k)
        return carry

    lax.fori_loop(0, i // UNROLL, group, 0)

    base = (i // UNROLL) * UNROLL
    for rem in range(UNROLL):
        @pl.when(i % UNROLL == rem)
        def _(rem=rem):
            for k in range(rem):
                step(base + k, k)
            last_step(i, rem, UNROLL if rem == UNROLL - 1 else rem)

    acc = acc_ref[...]
    o1 = acc[:tq, :HEAD_W] / acc[:tq, HEAD_W:]
    o2 = acc[tq:, :HEAD_W] / acc[tq:, HEAD_W:]
    o_ref[...] = _diff_merge(o1, o2, lam_ref[0], g_ref[...], lam0).astype(o_ref.dtype)


def _attn_prompt(q, k, v, lam, subln_g, lam0, tq):
    b, s, w = q.shape
    assert s % tq == 0 and tq % CHUNK == 0 and tq % LANES == 0 and w == N_HEADS * HEAD_W
    q_spec = pl.BlockSpec((None, tq, HEAD_W), lambda bi, h, i: (bi, i, h))
    kv_spec = pl.BlockSpec((None, s, HEAD_W), lambda bi, h, i: (bi, 0, h))
    return pl.pallas_call(
        functools.partial(_attn_prompt_kernel, tq=tq, lam0=lam0),
        grid=(b, N_HEADS, s // tq),
        in_specs=[
            pl.BlockSpec(memory_space=pltpu.SMEM),
            pl.BlockSpec((1, HEAD_W), lambda bi, h, i: (0, 0)),
            q_spec, kv_spec, kv_spec,
        ],
        out_specs=q_spec,
        out_shape=jax.ShapeDtypeStruct((b, s, w), BF16),
        scratch_shapes=[
            pltpu.VMEM((2 * tq, HEAD_W), BF16),
            pltpu.VMEM((UNROLL, 2 * tq, tq), F32),
            pltpu.VMEM((UNROLL + 1, 2 * tq, tq), BF16),
            pltpu.VMEM((UNROLL + 1, 2 * tq, LANES), F32),
            pltpu.VMEM((2 * tq, LANES), F32),
            pltpu.VMEM((2 * tq, 2 * HEAD_W), F32),
        ],
        compiler_params=_params("parallel", "parallel", "arbitrary"),
        name="attn_prompt",
    )(lam, subln_g, q, k, v)


def _attn_sample_kernel(lam_ref, g_ref, q_ref, k_ref, v_ref, o_ref, *, n_q, n_keys, lam0):
    qs = _stack_queries(q_ref[...])
    s = _scores(qs, k_ref[...])
    col = lax.broadcasted_iota(jnp.int32, s.shape, 1)
    s = jnp.where(col < n_keys, s, NEG_INF)
    m = jnp.max(s, axis=1, keepdims=True)
    p = jnp.exp(s - m)
    l = jnp.sum(p, axis=1, keepdims=True)
    o = jnp.dot(p.astype(BF16), v_ref[...], preferred_element_type=F32) / l
    o_ref[...] = _diff_merge(o[:n_q], o[n_q:], lam_ref[0], g_ref[...], lam0).astype(o_ref.dtype)


def _attn_sample(q, k_all, v_all, n_keys, lam, subln_g, lam0):
    b, n_q, w = q.shape
    n_pad = k_all.shape[1]
    q_spec = pl.BlockSpec((None, n_q, HEAD_W), lambda bi, h: (bi, 0, h))
    kv_spec = pl.BlockSpec((None, n_pad, HEAD_W), lambda bi, h: (bi, 0, h))
    return pl.pallas_call(
        functools.partial(_attn_sample_kernel, n_q=n_q, n_keys=n_keys, lam0=lam0),
        grid=(b, N_HEADS),
        in_specs=[
            pl.BlockSpec(memory_space=pltpu.SMEM),
            pl.BlockSpec((1, HEAD_W), lambda bi, h: (0, 0)),
            q_spec, kv_spec, kv_spec,
        ],
        out_specs=q_spec,
        out_shape=jax.ShapeDtypeStruct((b, n_q, w), BF16),
        compiler_params=_params("parallel", "parallel"),
        name="attn_sample",
    )(lam, subln_g, q, k_all, v_all)


def _merge_kernel(a_ref, u_ref, gv_ref, x_ref, wm_ref, bs_ref, wo_ref, g_ref, b_ref, y_ref, gated_ref,
                  *, tm, alpha):
    d_a = a_ref.shape[1]
    for c in range(tm // CHUNK_MLP):
        rows = slice(c * CHUNK_MLP, (c + 1) * CHUNK_MLP)
        for grp in range(N_GROUPS):
            cs = slice(grp * GROUP_W, (grp + 1) * GROUP_W)
            mixed = jnp.dot(wm_ref[grp], gv_ref[rows, cs].astype(BF16), preferred_element_type=F32)
            gated_ref[rows, cs] = (u_ref[rows, cs] * (mixed + bs_ref[:, cs])).astype(BF16)
    mix = jnp.dot(a_ref[...], wo_ref[:d_a, :], preferred_element_type=F32)
    mix = mix + jnp.dot(gated_ref[...], wo_ref[d_a:, :], preferred_element_type=F32)
    y_ref[...] = _layer_norm(alpha * x_ref[...] + mix, g_ref[...], b_ref[...])


def _merge(a, u, gvn, x, wm, bs, w_out, ln_g, ln_b, alpha, tm):
    t, d = x.shape
    d_a, d_b = a.shape[1], u.shape[1]
    assert t % tm == 0 and tm % CHUNK_MLP == 0
    row = lambda w: pl.BlockSpec((tm, w), lambda i: (i, 0))
    const = lambda shape: pl.BlockSpec(shape, lambda i: (0,) * len(shape))
    return pl.pallas_call(
        functools.partial(_merge_kernel, tm=tm, alpha=alpha),
        grid=(t // tm,),
        in_specs=[row(d_a), row(d_b), row(d_b), row(d), const(wm.shape), const(bs.shape),
                  const(w_out.shape), const((1, d)), const((1, d))],
        out_specs=row(d),
        out_shape=jax.ShapeDtypeStruct((t, d), F32),
        scratch_shapes=[pltpu.VMEM((tm, d_b), BF16)],
        compiler_params=_params("parallel"),
        name="merge",
    )(a, u, gvn, x, wm, bs, w_out, ln_g, ln_b)


def _ff_chunks(d_ff, width):
    return [(s, min(width, d_ff - s)) for s in range(0, d_ff, width)]


def _swiglu_tile(x, wg_ref, wu_ref, wd_ref, ff_chunk):
    acc = None
    for start, size in _ff_chunks(wg_ref.shape[1], ff_chunk):
        hg = jnp.dot(x, wg_ref[:, start:start + size], preferred_element_type=F32)
        hu = jnp.dot(x, wu_ref[:, start:start + size], preferred_element_type=F32)
        h = (jax.nn.silu(hg) * hu).astype(BF16)
        part = jnp.dot(h, wd_ref[start:start + size, :], preferred_element_type=F32)
        acc = part if acc is None else acc + part
    return acc


def _ffn_dense_kernel(x_ref, wg_ref, wu_ref, wd_ref, g_ref, b_ref, y_ref, *, alpha, ff_chunk):
    x = x_ref[...]
    ffn = _swiglu_tile(x.astype(BF16), wg_ref, wu_ref, wd_ref, ff_chunk)
    y_ref[...] = _layer_norm(alpha * x + ffn, g_ref[...], b_ref[...])


def _ffn_dense(x, wg, wu, wd, ln_g, ln_b, alpha, tm, ff_chunk):
    t, d = x.shape
    assert t % tm == 0
    row = pl.BlockSpec((tm, d), lambda i: (i, 0))
    const = lambda shape: pl.BlockSpec(shape, lambda i: (0, 0), pipeline_mode=pl.Buffered(1))
    return pl.pallas_call(
        functools.partial(_ffn_dense_kernel, alpha=alpha, ff_chunk=ff_chunk),
        grid=(t // tm,),
        in_specs=[row, const(wg.shape), const(wu.shape), const(wd.shape), const((1, d)), const((1, d))],
        out_specs=row,
        out_shape=jax.ShapeDtypeStruct((t, d), F32),
        compiler_params=_params("parallel"),
        name="ffn_dense",
    )(x, wg, wu, wd, ln_g, ln_b)


def _ffn_grouped_kernel(te_ref, nt_ref, x_ref, wg_ref, wu_ref, wd_ref, y_ref, *, ff_chunk):
    @pl.when(pl.program_id(0) < nt_ref[0])
    def _():
        y_ref[...] = _swiglu_tile(x_ref[...].astype(BF16), wg_ref, wu_ref, wd_ref, ff_chunk)


def _ffn_grouped(xs, tile_expert, n_tiles_used, wg, wu, wd, tm, ff_chunk):
    r, d = xs.shape
    assert r % tm == 0
    row = pl.BlockSpec((tm, d), lambda i, te, nt: (i, 0))
    expert = lambda shape: pl.BlockSpec((None,) + shape[1:], lambda i, te, nt: (te[i], 0, 0),
                                        pipeline_mode=pl.Buffered(1))
    return pl.pallas_call(
        functools.partial(_ffn_grouped_kernel, ff_chunk=ff_chunk),
        grid_spec=pltpu.PrefetchScalarGridSpec(
            num_scalar_prefetch=2,
            grid=(r // tm,),
            in_specs=[row, expert(wg.shape), expert(wu.shape), expert(wd.shape)],
            out_specs=row,
        ),
        out_shape=jax.ShapeDtypeStruct((r, d), F32),
        compiler_params=_params("arbitrary"),
        name="ffn_grouped",
    )(tile_expert, n_tiles_used, xs, wg, wu, wd)


def _router_kernel(x_ref, wr_ref, e_ref, g_ref, r_ref, c_ref):
    logits = lax.dot_general(wr_ref[...], x_ref[...], (((1,), (1,)), ((), ())),
                             precision=lax.Precision.HIGHEST, preferred_element_type=F32)
    tm = logits.shape[1]
    idx = lax.broadcasted_iota(jnp.int32, logits.shape, 0)
    m1 = jnp.max(logits, axis=0, keepdims=True)
    i1 = jnp.min(jnp.where(logits == m1, idx, N_EXPERTS), axis=0, keepdims=True)
    rest = jnp.where(idx == i1, -jnp.inf, logits)
    m2 = jnp.max(rest, axis=0, keepdims=True)
    i2 = jnp.min(jnp.where(rest == m2, idx, N_EXPERTS), axis=0, keepdims=True)
    e2 = jnp.exp(m2 - m1)
    denom = 1.0 + e2
    e_ref[...] = jnp.concatenate([i1, i2], axis=0)
    g_ref[...] = jnp.concatenate([1.0 / denom, e2 / denom], axis=0)
    hit1, hit2 = idx == i1, idx == i2
    hit = jnp.where(jnp.logical_or(hit1, hit2), 1.0, 0.0)
    earlier = jnp.where(lax.broadcasted_iota(jnp.int32, (tm, tm), 0) < lax.broadcasted_iota(jnp.int32, (tm, tm), 1),
                        1.0, 0.0).astype(BF16)
    before = jnp.dot(hit.astype(BF16), earlier, preferred_element_type=F32)
    r_ref[...] = jnp.concatenate([jnp.sum(jnp.where(hit1, before, 0.0), axis=0, keepdims=True),
                                  jnp.sum(jnp.where(hit2, before, 0.0), axis=0, keepdims=True)],
                                 axis=0).astype(jnp.int32)
    c_ref[...] = jnp.broadcast_to(jnp.sum(hit, axis=1, keepdims=True), c_ref.shape)


def _router(x, w_router_t, tm):
    t, d = x.shape
    assert t % tm == 0
    pair = pl.BlockSpec((2, tm), lambda i: (0, i))
    experts, gates, ranks, counts = pl.pallas_call(
        _router_kernel,
        grid=(t // tm,),
        in_specs=[pl.BlockSpec((tm, d), lambda i: (i, 0)), pl.BlockSpec(w_router_t.shape, lambda i: (0, 0))],
        out_specs=[pair, pair, pair, pl.BlockSpec((None, N_EXPERTS, LANES), lambda i: (i, 0, 0))],
        out_shape=[jax.ShapeDtypeStruct((2, t), jnp.int32), jax.ShapeDtypeStruct((2, t), F32),
                   jax.ShapeDtypeStruct((2, t), jnp.int32),
                   jax.ShapeDtypeStruct((t // tm, N_EXPERTS, LANES), F32)],
        compiler_params=_params("parallel"),
        name="router",
    )(x, w_router_t)
    return experts, gates, ranks, counts[:, :, 0].astype(jnp.int32)


SC_WINDOW = 128
SC_COLS = 256


def _sc_mesh():
    return plsc.VectorSubcoreMesh(core_axis_name="core", subcore_axis_name="subcore")


def _sc_pipeline(body, n_rows, in_specs, out_specs):
    return pltpu.emit_pipeline(body, grid=(n_rows // SC_WINDOW,), in_specs=in_specs, out_specs=out_specs,
                               core_axis_name=("core", "subcore"), dimension_semantics=(pltpu.PARALLEL,))


def _sc_dispatch(x, pos, n_rows):
    t, d = x.shape
    assert t % SC_WINDOW == 0 and d % SC_COLS == 0

    @pl.kernel(out_type=jax.ShapeDtypeStruct((n_rows, d), x.dtype), mesh=_sc_mesh())
    def dispatch(x_hbm, pos_hbm, o_hbm):
        for s in range(pos.shape[0]):
            for c in range(d // SC_COLS):
                def body(x_vmem, pos_vmem, c=c):
                    pltpu.sync_copy(x_vmem, o_hbm.at[pos_vmem.at[0], pl.ds(c * SC_COLS, SC_COLS)])

                _sc_pipeline(body, t,
                             [pl.BlockSpec((SC_WINDOW, SC_COLS), lambda i, c=c: (i, c)),
                              pl.BlockSpec((1, SC_WINDOW), lambda i, s=s: (s, i))],
                             [])(x_hbm, pos_hbm)

    return dispatch(x, pos)


def _sc_gather(y, pos):
    n = pos.shape[0]
    d = y.shape[1]
    assert n % SC_WINDOW == 0 and d % SC_COLS == 0

    @pl.kernel(out_type=jax.ShapeDtypeStruct((n, d), y.dtype), mesh=_sc_mesh())
    def gather(y_hbm, pos_hbm, o_hbm):
        for c in range(d // SC_COLS):
            def body(pos_vmem, o_vmem, c=c):
                pltpu.sync_copy(y_hbm.at[pos_vmem.at[0], pl.ds(c * SC_COLS, SC_COLS)], o_vmem)

            _sc_pipeline(body, n,
                         [pl.BlockSpec((1, SC_WINDOW), lambda i: (0, i))],
                         [pl.BlockSpec((SC_WINDOW, SC_COLS), lambda i, c=c: (i, c))])(pos_hbm, o_hbm)

    return gather(y, pos.reshape(1, n))


def _combine_kernel(x_ref, y1_ref, y2_ref, g1_ref, g2_ref, lg_ref, lb_ref, oa_ref, ob_ref, *, alpha, n_a):
    ffn = g1_ref[...] * y1_ref[...] + g2_ref[...] * y2_ref[...]
    out = _layer_norm(alpha * x_ref[...] + ffn, lg_ref[...], lb_ref[...])

    @pl.when(pl.program_id(0) < n_a)
    def _():
        oa_ref[...] = out

    @pl.when(pl.program_id(0) >= n_a)
    def _():
        ob_ref[...] = out


def _combine(x, y_pairs, g1, g2, ln_g, ln_b, alpha, tm, t_a):
    t, d = x.shape
    n_blk, n_a = t // tm, t_a // tm
    assert t % tm == 0 and t_a % tm == 0 and 0 < n_a < n_blk
    row = pl.BlockSpec((tm, d), lambda i: (i, 0))
    row2 = pl.BlockSpec((tm, d), lambda i: (i + n_blk, 0))
    col = pl.BlockSpec((tm, 1), lambda i: (i, 0))
    const = pl.BlockSpec((1, d), lambda i: (0, 0))
    return pl.pallas_call(
        functools.partial(_combine_kernel, alpha=alpha, n_a=n_a),
        grid=(n_blk,),
        in_specs=[row, row, row2, col, col, const, const],
        out_specs=[pl.BlockSpec((tm, d), lambda i: (jnp.minimum(i, n_a - 1), 0)),
                   pl.BlockSpec((tm, d), lambda i: (jnp.maximum(i - n_a, 0), 0))],
        out_shape=[jax.ShapeDtypeStruct((t_a, d), F32), jax.ShapeDtypeStruct((t - t_a, d), F32)],
        compiler_params=_params("arbitrary"),
        name="combine",
    )(x, y_pairs, y_pairs, g1, g2, ln_g, ln_b)


def _moe(x, t_a, w_router_t, wg, wu, wd, ln_g, ln_b, alpha, tm_route, tm_ffn, ff_chunk):
    t, d = x.shape
    n_blk = t // tm_route
    experts, gates, ranks, counts = _router(x, w_router_t, tm_route)
    total = jnp.sum(counts, axis=0)
    tiles_per = (total + tm_ffn - 1) // tm_ffn
    tile_end = jnp.cumsum(tiles_per)
    starts = (tile_end - tiles_per) * tm_ffn
    base = starts[None, :] + jnp.cumsum(counts, axis=0) - counts
    experts3 = experts.reshape(2, n_blk, tm_route)
    pos = ranks.reshape(2, n_blk, tm_route)
    for e in range(N_EXPERTS):
        pos = pos + jnp.where(experts3 == e, base[None, :, e, None], 0)
    pos = pos.reshape(2, t)
    n_tiles = (2 * t) // tm_ffn + N_EXPERTS
    tile_ids = jnp.arange(n_tiles, dtype=jnp.int32)
    tile_expert = jnp.minimum(jnp.sum((tile_end[None, :] <= tile_ids[:, None]).astype(jnp.int32), axis=1),
                              N_EXPERTS - 1)
    xs = _sc_dispatch(x, pos, n_tiles * tm_ffn)
    ys = _ffn_grouped(xs, tile_expert, tile_end[-1:].astype(jnp.int32), wg, wu, wd, tm_ffn, ff_chunk)
    y_pairs = _sc_gather(ys, pos.reshape(2 * t))
    return _combine(x, y_pairs, gates[0][:, None], gates[1][:, None], ln_g, ln_b, alpha, tm_route, t_a)


def _spatial_params(w_s, b_s, n_pos):
    reps = CHUNK_MLP // n_pos
    wm = w_s[:, :n_pos, :n_pos] * jnp.tril(jnp.ones((n_pos, n_pos), w_s.dtype))
    eye = jnp.eye(reps, dtype=w_s.dtype)
    wm = jnp.einsum("ab,gij->gaibj", eye, wm).reshape(N_GROUPS, CHUNK_MLP, CHUNK_MLP)
    bias = jnp.tile(b_s[:, :n_pos].T, (reps, 1))
    return wm.astype(BF16), jnp.repeat(bias, GROUP_W, axis=1).astype(F32)


def kernel(x_prompt, x_sample, cache_k, cache_v, w_in, lam_q1, lam_k1, lam_q2, lam_k2, subln_g,
           sgu_ln_g, sgu_ln_b, w_spatial, b_spatial, w_out, ln_mix_g, ln_mix_b,
           w_ffn_gate, w_ffn_up, w_ffn_down, w_router, w_moe_gate, w_moe_up, w_moe_down,
           ln_ffn_g, ln_ffn_b):
    depth = w_in.shape[0]
    batch, seq, d_model = x_prompt.shape
    dec_batch, dec_seq, _ = x_sample.shape
    past_len = cache_k.shape[2]
    alpha = (2.0 * depth) ** 0.25
    t_p, t_s = batch * seq, dec_batch * dec_seq
    n_keys = past_len + dec_seq
    keys_pad = -(-n_keys // LANES) * LANES

    tm_p = min(512, t_p)
    tm_s = min(512, t_s)
    tm_moe = math.gcd(tm_p, tm_s)
    tq = min(512, seq)

    xp = x_prompt.reshape(t_p, d_model)
    xs = x_sample.reshape(t_s, d_model)
    k_p, k_s, gv_s, v_p, v_s = [], [], [], [], []
    row2 = lambda a: a.reshape(1, -1).astype(F32)

    for l in range(depth):
        lam0 = _lambda_init(l)
        lam = (jnp.exp(jnp.sum(lam_q1[l].astype(F32) * lam_k1[l].astype(F32)))
               - jnp.exp(jnp.sum(lam_q2[l].astype(F32) * lam_k2[l].astype(F32))) + lam0).reshape(1)
        w_in_l = _cast_bf16(w_in, l)
        w_out_l = _cast_bf16(w_out, l)
        sub_g = row2(subln_g[l])
        sgu_g, sgu_b = row2(sgu_ln_g[l]), row2(sgu_ln_b[l])
        mix_g, mix_b = row2(ln_mix_g[l]), row2(ln_mix_b[l])
        ffn_g, ffn_b = row2(ln_ffn_g[l]), row2(ln_ffn_b[l])
        wm_p, bs_p = _spatial_params(w_spatial[l], b_spatial[l], CHUNK_MLP)
        wm_s, bs_s = _spatial_params(w_spatial[l], b_spatial[l], dec_seq)

        def channel_mix(x_p, x_s):
            i = l // 2
            if l % 2 == 0:
                wg, wu, wd = (_cast_bf16(w, i) for w in (w_ffn_gate, w_ffn_up, w_ffn_down))
                dense = lambda x, tm: _ffn_dense(x, wg, wu, wd, ffn_g, ffn_b, alpha, tm, 512)
                return dense(x_p, tm_p), dense(x_s, tm_s)
            wg, wu, wd = (_cast_bf16(w, i) for w in (w_moe_gate, w_moe_up, w_moe_down))
            return _moe(jnp.concatenate([x_p, x_s], axis=0), t_p, w_router[i].T.astype(F32), wg, wu, wd,
                        ffn_g, ffn_b, alpha, tm_moe, tm_moe, 512)

        q, k, v, kb, vb, u, gvn = _proj(xp, w_in_l, sgu_g, sgu_b, tm_p, BF16)
        k_p.append(k)
        v_p.append(v)
        shape3 = (batch, seq, -1)
        a = _attn_prompt(q.reshape(shape3), kb.reshape(shape3), vb.reshape(shape3), lam, sub_g, lam0, tq)
        xp = _merge(a.reshape(t_p, -1), u, gvn, xp, wm_p, bs_p, w_out_l, mix_g, mix_b, alpha, tm_p)

        q, k, v, kb, vb, u, gvn = _proj(xs, w_in_l, sgu_g, sgu_b, tm_s, F32)
        k_s.append(k)
        v_s.append(v)
        pad = lambda c, new: jnp.pad(
            jnp.concatenate([c.reshape(dec_batch, past_len, -1).astype(BF16),
                             new.reshape(dec_batch, dec_seq, -1)], axis=1),
            ((0, 0), (0, keys_pad - n_keys), (0, 0)))
        a = _attn_sample(q.reshape(dec_batch, dec_seq, -1), pad(cache_k[l], kb), pad(cache_v[l], vb),
                         n_keys, lam, sub_g, lam0)
        xs = _merge(a.reshape(t_s, -1), u, gvn, xs, wm_s, bs_s, w_out_l, mix_g, mix_b, alpha, tm_s)
        gv_s.append(gvn.reshape(dec_batch, dec_seq, -1))

        xp, xs = channel_mix(xp, xs)

    return (xp.reshape(batch, seq, d_model), xs.reshape(dec_batch, dec_seq, d_model),
            jnp.stack([k.reshape(batch, seq, N_HEADS, 2, D_HEAD) for k in k_p], axis=0),
            jnp.stack([v.reshape(batch, seq, N_HEADS, HEAD_W) for v in v_p], axis=0),
            jnp.stack([k.reshape(dec_batch, dec_seq, N_HEADS, 2, D_HEAD) for k in k_s], axis=0),
            jnp.stack([v.reshape(dec_batch, dec_seq, N_HEADS, HEAD_W) for v in v_s], axis=0),
            jnp.stack(gv_s, axis=0))
```

```python
import functools
import math

import jax
import jax.numpy as jnp
from jax import lax
from jax.experimental import pallas as pl
from jax.experimental.pallas import tpu as pltpu
from jax.experimental.pallas import tpu_sc as plsc

F32 = jnp.float32
BF16 = jnp.bfloat16

N_HEADS = 4
HEAD_W = 128
D_HEAD = 64
N_GROUPS = 4
GROUP_W = 128
CHUNK = 64
CHUNK_MLP = 128
N_EXPERTS = 8
LN_EPS = 1e-5
NEG_INF = -1e30

LANES = 128
MXU_DIM = 256
VMEM_LIMIT_BYTES = 56 * 1024 * 1024
CAST_ROWS = 256
UNROLL = 8


def _lambda_init(layer):
    return 0.8 - 0.6 * math.exp(-0.3 * layer)


def _layer_norm(x, g, b):
    mu = jnp.mean(x, axis=-1, keepdims=True)
    xc = x - mu
    var = jnp.mean(xc * xc, axis=-1, keepdims=True)
    return xc * lax.rsqrt(var + LN_EPS) * g + b


def _params(*semantics):
    return pltpu.CompilerParams(dimension_semantics=semantics, vmem_limit_bytes=VMEM_LIMIT_BYTES)


def _cast_kernel(w_ref, o_ref):
    o_ref[...] = w_ref[...].astype(o_ref.dtype)


def _cast_bf16(w, index):
    r, c = w.shape[-2:]
    w4 = w.reshape(w.shape[0], -1, r, c)
    rows = math.gcd(r, CAST_ROWS)
    out = pl.pallas_call(
        _cast_kernel,
        grid=(w4.shape[1], r // rows),
        in_specs=[pl.BlockSpec((None, None, rows, c), lambda e, j: (index, e, j, 0))],
        out_specs=pl.BlockSpec((None, rows, c), lambda e, j: (e, j, 0)),
        out_shape=jax.ShapeDtypeStruct(w4.shape[1:], BF16),
        compiler_params=_params("parallel", "parallel"),
        name="cast_bf16",
    )(w4)
    return out.reshape(w.shape[1:])


def _proj_kernel(x_ref, w_ref, g_ref, b_ref, q_ref, k_ref, v_ref, kb_ref, vb_ref, u_ref, gv_ref,
                 *, qk_w, d_a, d_b):
    x = x_ref[...].astype(BF16)

    def cols(start, width):
        return jnp.dot(x, w_ref[:, start:start + width], preferred_element_type=F32)

    q_ref[...] = (cols(0, qk_w) * (D_HEAD ** -0.5)).astype(BF16)
    k = cols(qk_w, qk_w)
    k_ref[...] = k
    kb_ref[...] = k.astype(BF16)
    v = cols(2 * qk_w, d_a)
    v_ref[...] = v
    vb_ref[...] = v.astype(BF16)
    off = 2 * qk_w + d_a
    u_ref[...] = jax.nn.gelu(cols(off, d_b)).astype(u_ref.dtype)
    gv = jax.nn.gelu(cols(off + d_b, d_b))
    gv_ref[...] = _layer_norm(gv, g_ref[...], b_ref[...]).astype(gv_ref.dtype)


def _proj(x, w_in, ln_g, ln_b, tm, gv_dtype):
    t, d = x.shape
    qk_w = N_HEADS * HEAD_W
    d_a = N_HEADS * HEAD_W
    d_b = N_GROUPS * GROUP_W
    assert w_in.shape == (d, 2 * qk_w + d_a + 2 * d_b) and t % tm == 0
    row = lambda w: pl.BlockSpec((tm, w), lambda i: (i, 0))
    const = lambda shape: pl.BlockSpec(shape, lambda i: (0, 0))
    return pl.pallas_call(
        functools.partial(_proj_kernel, qk_w=qk_w, d_a=d_a, d_b=d_b),
        grid=(t // tm,),
        in_specs=[row(d), const(w_in.shape), const((1, d_b)), const((1, d_b))],
        out_specs=[row(qk_w), row(qk_w), row(d_a), row(qk_w), row(d_a), row(d_b), row(d_b)],
        out_shape=[
            jax.ShapeDtypeStruct((t, qk_w), BF16),
            jax.ShapeDtypeStruct((t, qk_w), F32),
            jax.ShapeDtypeStruct((t, d_a), F32),
            jax.ShapeDtypeStruct((t, qk_w), BF16),
            jax.ShapeDtypeStruct((t, d_a), BF16),
            jax.ShapeDtypeStruct((t, d_b), BF16),
            jax.ShapeDtypeStruct((t, d_b), gv_dtype),
        ],
        compiler_params=_params("parallel"),
        name="proj",
    )(x, w_in, ln_g, ln_b)


def _stack_queries(q):
    lane = lax.broadcasted_iota(jnp.int32, q.shape, 1)
    zero = jnp.zeros_like(q)
    return jnp.concatenate([jnp.where(lane < D_HEAD, q, zero), jnp.where(lane >= D_HEAD, q, zero)], axis=0)


def _scores(qs, kb):
    return lax.dot_general(qs, kb, (((1,), (1,)), ((), ())), preferred_element_type=F32)


def _diff_merge(o1, o2, lam, g, lam0):
    o = o1 - lam * o2
    ms = jnp.mean(o * o, axis=-1, keepdims=True)
    return o * lax.rsqrt(ms + LN_EPS) * g * (1.0 - lam0)


def _attn_prompt_kernel(lam_ref, g_ref, q_ref, k_ref, v_ref, o_ref,
                        qs_ref, s_ref, p_ref, a_ref, m_ref, acc_ref, *, tq, lam0):
    i = pl.program_id(2)
    n_rep = tq // LANES
    diag_slot = UNROLL - 1
    qs_ref[...] = _stack_queries(q_ref[...])
    m_ref[...] = jnp.full(m_ref.shape, NEG_INF, F32)
    acc_ref[...] = jnp.zeros(acc_ref.shape, F32)
    ones = jnp.ones((tq, HEAD_W), BF16)

    def scores(j, slot):
        off = pl.multiple_of(j * tq, tq)
        s_ref[slot] = _scores(qs_ref[...], k_ref[pl.ds(off, tq), :])

    def softmax(slot, masked):
        for r0 in range(0, 2 * tq, CHUNK):
            rows = slice(r0, r0 + CHUNK)
            s = s_ref[slot, rows, :]
            if masked:
                n_vis = (r0 % tq) + CHUNK
                s = jnp.where(lax.broadcasted_iota(jnp.int32, s.shape, 1) < n_vis, s, NEG_INF)
            m_prev = m_ref[rows, :]
            m_new = jnp.maximum(m_prev, jnp.max(s, axis=1, keepdims=True))
            a_ref[slot, rows, :] = jnp.exp(m_prev - m_new)
            p_ref[slot, rows, :] = jnp.exp(s - jnp.concatenate([m_new] * n_rep, axis=1)).astype(BF16)
            m_ref[rows, :] = m_new

    def pv(j, slot):
        off = pl.multiple_of(j * tq, tq)
        v_ext = jnp.concatenate([v_ref[pl.ds(off, tq), :], ones], axis=1)
        a = a_ref[slot]
        acc_ref[...] = (jnp.concatenate([a, a], axis=1) * acc_ref[...]
                        + jnp.dot(p_ref[slot], v_ext, preferred_element_type=F32))

    def step(t, slot, next_scores=True):
        if next_scores:
            scores(t + 1, (slot + 1) % UNROLL)
        pv(jnp.where(t == 0, i, t - 1), (slot - 1) % UNROLL)
        softmax(slot, False)

    scores(i, diag_slot)
    scores(0, 0)
    softmax(diag_slot, True)

    def group(w, carry):
        for k in range(UNROLL):
            step(UNROLL * w + k, k)
        return carry

    lax.fori_loop(0, i // UNROLL, group, 0)

    base = (i // UNROLL) * UNROLL
    for rem in range(UNROLL):
        @pl.when(i % UNROLL == rem)
        def _(rem=rem):
            for k in range(rem):
                step(base + k, k, next_scores=k < rem - 1)
            pv(jnp.maximum(i - 1, 0), (rem - 1) % UNROLL)

    acc = acc_ref[...]
    o1 = acc[:tq, :HEAD_W] / acc[:tq, HEAD_W:]
    o2 = acc[tq:, :HEAD_W] / acc[tq:, HEAD_W:]
    o_ref[...] = _diff_merge(o1, o2, lam_ref[0], g_ref[...], lam0).astype(o_ref.dtype)


def _attn_prompt(q, k, v, lam, subln_g, lam0, tq):
    b, s, w = q.shape
    assert s % tq == 0 and tq % CHUNK == 0 and tq % LANES == 0 and w == N_HEADS * HEAD_W
    q_spec = pl.BlockSpec((None, tq, HEAD_W), lambda bi, h, i: (bi, i, h))
    kv_spec = pl.BlockSpec((None, s, HEAD_W), lambda bi, h, i: (bi, 0, h))
    return pl.pallas_call(
        functools.partial(_attn_prompt_kernel, tq=tq, lam0=lam0),
        grid=(b, N_HEADS, s // tq),
        in_specs=[
            pl.BlockSpec(memory_space=pltpu.SMEM),
            pl.BlockSpec((1, HEAD_W), lambda bi, h, i: (0, 0)),
            q_spec, kv_spec, kv_spec,
        ],
        out_specs=q_spec,
        out_shape=jax.ShapeDtypeStruct((b, s, w), BF16),
        scratch_shapes=[
            pltpu.VMEM((2 * tq, HEAD_W), BF16),
            pltpu.VMEM((UNROLL, 2 * tq, tq), F32),
            pltpu.VMEM((UNROLL, 2 * tq, tq), BF16),
            pltpu.VMEM((UNROLL, 2 * tq, LANES), F32),
            pltpu.VMEM((2 * tq, LANES), F32),
            pltpu.VMEM((2 * tq, 2 * HEAD_W), F32),
        ],
        compiler_params=_params("parallel", "parallel", "arbitrary"),
        name="attn_prompt",
    )(lam, subln_g, q, k, v)


def _attn_sample_kernel(lam_ref, g_ref, q_ref, k_ref, v_ref, o_ref, *, n_q, n_keys, lam0):
    qs = _stack_queries(q_ref[...])
    s = _scores(qs, k_ref[...])
    col = lax.broadcasted_iota(jnp.int32, s.shape, 1)
    s = jnp.where(col < n_keys, s, NEG_INF)
    m = jnp.max(s, axis=1, keepdims=True)
    p = jnp.exp(s - m)
    l = jnp.sum(p, axis=1, keepdims=True)
    o = jnp.dot(p.astype(BF16), v_ref[...], preferred_element_type=F32) / l
    o_ref[...] = _diff_merge(o[:n_q], o[n_q:], lam_ref[0], g_ref[...], lam0).astype(o_ref.dtype)


def _attn_sample(q, k_all, v_all, n_keys, lam, subln_g, lam0):
    b, n_q, w = q.shape
    n_pad = k_all.shape[1]
    q_spec = pl.BlockSpec((None, n_q, HEAD_W), lambda bi, h: (bi, 0, h))
    kv_spec = pl.BlockSpec((None, n_pad, HEAD_W), lambda bi, h: (bi, 0, h))
    return pl.pallas_call(
        functools.partial(_attn_sample_kernel, n_q=n_q, n_keys=n_keys, lam0=lam0),
        grid=(b, N_HEADS),
        in_specs=[
            pl.BlockSpec(memory_space=pltpu.SMEM),
            pl.BlockSpec((1, HEAD_W), lambda bi, h: (0, 0)),
            q_spec, kv_spec, kv_spec,
        ],
        out_specs=q_spec,
        out_shape=jax.ShapeDtypeStruct((b, n_q, w), BF16),
        compiler_params=_params("parallel", "parallel"),
        name="attn_sample",
    )(lam, subln_g, q, k_all, v_all)


def _merge_kernel(a_ref, u_ref, gv_ref, x_ref, wm_ref, bs_ref, wo_ref, g_ref, b_ref, y_ref, gated_ref,
                  *, tm, alpha):
    d_a = a_ref.shape[1]
    for c in range(tm // CHUNK_MLP):
        rows = slice(c * CHUNK_MLP, (c + 1) * CHUNK_MLP)
        for grp in range(N_GROUPS):
            cs = slice(grp * GROUP_W, (grp + 1) * GROUP_W)
            mixed = jnp.dot(wm_ref[grp], gv_ref[rows, cs].astype(BF16), preferred_element_type=F32)
            gated_ref[rows, cs] = (u_ref[rows, cs] * (mixed + bs_ref[:, cs])).astype(BF16)
    mix = jnp.dot(a_ref[...], wo_ref[:d_a, :], preferred_element_type=F32)
    mix = mix + jnp.dot(gated_ref[...], wo_ref[d_a:, :], preferred_element_type=F32)
    y_ref[...] = _layer_norm(alpha * x_ref[...] + mix, g_ref[...], b_ref[...])


def _merge(a, u, gvn, x, wm, bs, w_out, ln_g, ln_b, alpha, tm):
    t, d = x.shape
    d_a, d_b = a.shape[1], u.shape[1]
    assert t % tm == 0 and tm % CHUNK_MLP == 0
    row = lambda w: pl.BlockSpec((tm, w), lambda i: (i, 0))
    const = lambda shape: pl.BlockSpec(shape, lambda i: (0,) * len(shape))
    return pl.pallas_call(
        functools.partial(_merge_kernel, tm=tm, alpha=alpha),
        grid=(t // tm,),
        in_specs=[row(d_a), row(d_b), row(d_b), row(d), const(wm.shape), const(bs.shape),
                  const(w_out.shape), const((1, d)), const((1, d))],
        out_specs=row(d),
        out_shape=jax.ShapeDtypeStruct((t, d), F32),
        scratch_shapes=[pltpu.VMEM((tm, d_b), BF16)],
        compiler_params=_params("parallel"),
        name="merge",
    )(a, u, gvn, x, wm, bs, w_out, ln_g, ln_b)


def _ff_chunks(d_ff, width):
    return [(s, min(width, d_ff - s)) for s in range(0, d_ff, width)]


def _swiglu_tile(x, wg_ref, wu_ref, wd_ref, ff_chunk):
    acc = None
    for start, size in _ff_chunks(wg_ref.shape[1], ff_chunk):
        hg = jnp.dot(x, wg_ref[:, start:start + size], preferred_element_type=F32)
        hu = jnp.dot(x, wu_ref[:, start:start + size], preferred_element_type=F32)
        h = (jax.nn.silu(hg) * hu).astype(BF16)
        part = jnp.dot(h, wd_ref[start:start + size, :], preferred_element_type=F32)
        acc = part if acc is None else acc + part
    return acc


def _ffn_dense_kernel(x_ref, wg_ref, wu_ref, wd_ref, g_ref, b_ref, y_ref, *, alpha, ff_chunk):
    x = x_ref[...]
    ffn = _swiglu_tile(x.astype(BF16), wg_ref, wu_ref, wd_ref, ff_chunk)
    y_ref[...] = _layer_norm(alpha * x + ffn, g_ref[...], b_ref[...])


def _ffn_dense(x, wg, wu, wd, ln_g, ln_b, alpha, tm, ff_chunk):
    t, d = x.shape
    assert t % tm == 0
    row = pl.BlockSpec((tm, d), lambda i: (i, 0))
    const = lambda shape: pl.BlockSpec(shape, lambda i: (0, 0), pipeline_mode=pl.Buffered(1))
    return pl.pallas_call(
        functools.partial(_ffn_dense_kernel, alpha=alpha, ff_chunk=ff_chunk),
        grid=(t // tm,),
        in_specs=[row, const(wg.shape), const(wu.shape), const(wd.shape), const((1, d)), const((1, d))],
        out_specs=row,
        out_shape=jax.ShapeDtypeStruct((t, d), F32),
        compiler_params=_params("parallel"),
        name="ffn_dense",
    )(x, wg, wu, wd, ln_g, ln_b)


def _ffn_grouped_kernel(te_ref, nt_ref, x_ref, wg_ref, wu_ref, wd_ref, y_ref, *, ff_chunk):
    @pl.when(pl.program_id(0) < nt_ref[0])
    def _():
        y_ref[...] = _swiglu_tile(x_ref[...].astype(BF16), wg_ref, wu_ref, wd_ref, ff_chunk)


def _ffn_grouped(xs, tile_expert, n_tiles_used, wg, wu, wd, tm, ff_chunk):
    r, d = xs.shape
    assert r % tm == 0
    row = pl.BlockSpec((tm, d), lambda i, te, nt: (i, 0))
    expert = lambda shape: pl.BlockSpec((None,) + shape[1:], lambda i, te, nt: (te[i], 0, 0),
                                        pipeline_mode=pl.Buffered(1))
    return pl.pallas_call(
        functools.partial(_ffn_grouped_kernel, ff_chunk=ff_chunk),
        grid_spec=pltpu.PrefetchScalarGridSpec(
            num_scalar_prefetch=2,
            grid=(r // tm,),
            in_specs=[row, expert(wg.shape), expert(wu.shape), expert(wd.shape)],
            out_specs=row,
        ),
        out_shape=jax.ShapeDtypeStruct((r, d), F32),
        compiler_params=_params("arbitrary"),
        name="ffn_grouped",
    )(tile_expert, n_tiles_used, xs, wg, wu, wd)


def _router_kernel(x_ref, wr_ref, e_ref, g_ref, r_ref, c_ref):
    logits = lax.dot_general(wr_ref[...], x_ref[...], (((1,), (1,)), ((), ())),
                             precision=lax.Precision.HIGHEST, preferred_element_type=F32)
    tm = logits.shape[1]
    idx = lax.broadcasted_iota(jnp.int32, logits.shape, 0)
    m1 = jnp.max(logits, axis=0, keepdims=True)
    i1 = jnp.min(jnp.where(logits == m1, idx, N_EXPERTS), axis=0, keepdims=True)
    rest = jnp.where(idx == i1, -jnp.inf, logits)
    m2 = jnp.max(rest, axis=0, keepdims=True)
    i2 = jnp.min(jnp.where(rest == m2, idx, N_EXPERTS), axis=0, keepdims=True)
    e2 = jnp.exp(m2 - m1)
    denom = 1.0 + e2
    e_ref[...] = jnp.concatenate([i1, i2], axis=0)
    g_ref[...] = jnp.concatenate([1.0 / denom, e2 / denom], axis=0)
    hit1, hit2 = idx == i1, idx == i2
    hit = jnp.where(jnp.logical_or(hit1, hit2), 1.0, 0.0)
    earlier = jnp.where(lax.broadcasted_iota(jnp.int32, (tm, tm), 0) < lax.broadcasted_iota(jnp.int32, (tm, tm), 1),
                        1.0, 0.0).astype(BF16)
    before = jnp.dot(hit.astype(BF16), earlier, preferred_element_type=F32)
    r_ref[...] = jnp.concatenate([jnp.sum(jnp.where(hit1, before, 0.0), axis=0, keepdims=True),
                                  jnp.sum(jnp.where(hit2, before, 0.0), axis=0, keepdims=True)],
                                 axis=0).astype(jnp.int32)
    c_ref[...] = jnp.broadcast_to(jnp.sum(hit, axis=1, keepdims=True), c_ref.shape)


def _router(x, w_router_t, tm):
    t, d = x.shape
    assert t % tm == 0
    pair = pl.BlockSpec((2, tm), lambda i: (0, i))
    experts, gates, ranks, counts = pl.pallas_call(
        _router_kernel,
        grid=(t // tm,),
        in_specs=[pl.BlockSpec((tm, d), lambda i: (i, 0)), pl.BlockSpec(w_router_t.shape, lambda i: (0, 0))],
        out_specs=[pair, pair, pair, pl.BlockSpec((None, N_EXPERTS, LANES), lambda i: (i, 0, 0))],
        out_shape=[jax.ShapeDtypeStruct((2, t), jnp.int32), jax.ShapeDtypeStruct((2, t), F32),
                   jax.ShapeDtypeStruct((2, t), jnp.int32),
                   jax.ShapeDtypeStruct((t // tm, N_EXPERTS, LANES), F32)],
        compiler_params=_params("parallel"),
        name="router",
    )(x, w_router_t)
    return experts, gates, ranks, counts[:, :, 0].astype(jnp.int32)


SC_WINDOW = 128
SC_COLS = 256


def _sc_mesh():
    return plsc.VectorSubcoreMesh(core_axis_name="core", subcore_axis_name="subcore")


def _sc_pipeline(body, n_rows, in_specs, out_specs):
    return pltpu.emit_pipeline(body, grid=(n_rows // SC_WINDOW,), in_specs=in_specs, out_specs=out_specs,
                               core_axis_name=("core", "subcore"), dimension_semantics=(pltpu.PARALLEL,))


def _sc_dispatch(x, pos, n_rows):
    t, d = x.shape
    assert t % SC_WINDOW == 0 and d % SC_COLS == 0

    @pl.kernel(out_type=jax.ShapeDtypeStruct((n_rows, d), x.dtype), mesh=_sc_mesh())
    def dispatch(x_hbm, pos_hbm, o_hbm):
        for s in range(pos.shape[0]):
            for c in range(d // SC_COLS):
                def body(x_vmem, pos_vmem, c=c):
                    pltpu.sync_copy(x_vmem, o_hbm.at[pos_vmem.at[0], pl.ds(c * SC_COLS, SC_COLS)])

                _sc_pipeline(body, t,
                             [pl.BlockSpec((SC_WINDOW, SC_COLS), lambda i, c=c: (i, c)),
                              pl.BlockSpec((1, SC_WINDOW), lambda i, s=s: (s, i))],
                             [])(x_hbm, pos_hbm)

    return dispatch(x, pos)


def _sc_gather(y, pos):
    n = pos.shape[0]
    d = y.shape[1]
    assert n % SC_WINDOW == 0 and d % SC_COLS == 0

    @pl.kernel(out_type=jax.ShapeDtypeStruct((n, d), y.dtype), mesh=_sc_mesh())
    def gather(y_hbm, pos_hbm, o_hbm):
        for c in range(d // SC_COLS):
            def body(pos_vmem, o_vmem, c=c):
                pltpu.sync_copy(y_hbm.at[pos_vmem.at[0], pl.ds(c * SC_COLS, SC_COLS)], o_vmem)

            _sc_pipeline(body, n,
                         [pl.BlockSpec((1, SC_WINDOW), lambda i: (0, i))],
                         [pl.BlockSpec((SC_WINDOW, SC_COLS), lambda i, c=c: (i, c))])(pos_hbm, o_hbm)

    return gather(y, pos.reshape(1, n))


def _combine_kernel(x_ref, y1_ref, y2_ref, g1_ref, g2_ref, lg_ref, lb_ref, oa_ref, ob_ref, *, alpha, n_a):
    ffn = g1_ref[...] * y1_ref[...] + g2_ref[...] * y2_ref[...]
    out = _layer_norm(alpha * x_ref[...] + ffn, lg_ref[...], lb_ref[...])

    @pl.when(pl.program_id(0) < n_a)
    def _():
        oa_ref[...] = out

    @pl.when(pl.program_id(0) >= n_a)
    def _():
        ob_ref[...] = out


def _combine(x, y_pairs, g1, g2, ln_g, ln_b, alpha, tm, t_a):
    t, d = x.shape
    n_blk, n_a = t // tm, t_a // tm
    assert t % tm == 0 and t_a % tm == 0 and 0 < n_a < n_blk
    row = pl.BlockSpec((tm, d), lambda i: (i, 0))
    row2 = pl.BlockSpec((tm, d), lambda i: (i + n_blk, 0))
    col = pl.BlockSpec((tm, 1), lambda i: (i, 0))
    const = pl.BlockSpec((1, d), lambda i: (0, 0))
    return pl.pallas_call(
        functools.partial(_combine_kernel, alpha=alpha, n_a=n_a),
        grid=(n_blk,),
        in_specs=[row, row, row2, col, col, const, const],
        out_specs=[pl.BlockSpec((tm, d), lambda i: (jnp.minimum(i, n_a - 1), 0)),
                   pl.BlockSpec((tm, d), lambda i: (jnp.maximum(i - n_a, 0), 0))],
        out_shape=[jax.ShapeDtypeStruct((t_a, d), F32), jax.ShapeDtypeStruct((t - t_a, d), F32)],
        compiler_params=_params("arbitrary"),
        name="combine",
    )(x, y_pairs, y_pairs, g1, g2, ln_g, ln_b)


def _moe(x, t_a, w_router_t, wg, wu, wd, ln_g, ln_b, alpha, tm_route, tm_ffn, ff_chunk):
    t, d = x.shape
    n_blk = t // tm_route
    experts, gates, ranks, counts = _router(x, w_router_t, tm_route)
    total = jnp.sum(counts, axis=0)
    tiles_per = (total + tm_ffn - 1) // tm_ffn
    tile_end = jnp.cumsum(tiles_per)
    starts = (tile_end - tiles_per) * tm_ffn
    base = starts[None, :] + jnp.cumsum(counts, axis=0) - counts
    experts3 = experts.reshape(2, n_blk, tm_route)
    pos = ranks.reshape(2, n_blk, tm_route)
    for e in range(N_EXPERTS):
        pos = pos + jnp.where(experts3 == e, base[None, :, e, None], 0)
    pos = pos.reshape(2, t)
    n_tiles = (2 * t) // tm_ffn + N_EXPERTS
    tile_ids = jnp.arange(n_tiles, dtype=jnp.int32)
    tile_expert = jnp.minimum(jnp.sum((tile_end[None, :] <= tile_ids[:, None]).astype(jnp.int32), axis=1),
                              N_EXPERTS - 1)
    xs = _sc_dispatch(x, pos, n_tiles * tm_ffn)
    ys = _ffn_grouped(xs, tile_expert, tile_end[-1:].astype(jnp.int32), wg, wu, wd, tm_ffn, ff_chunk)
    y_pairs = _sc_gather(ys, pos.reshape(2 * t))
    return _combine(x, y_pairs, gates[0][:, None], gates[1][:, None], ln_g, ln_b, alpha, tm_route, t_a)


def _spatial_params(w_s, b_s, n_pos):
    reps = CHUNK_MLP // n_pos
    wm = w_s[:, :n_pos, :n_pos] * jnp.tril(jnp.ones((n_pos, n_pos), w_s.dtype))
    eye = jnp.eye(reps, dtype=w_s.dtype)
    wm = jnp.einsum("ab,gij->gaibj", eye, wm).reshape(N_GROUPS, CHUNK_MLP, CHUNK_MLP)
    bias = jnp.tile(b_s[:, :n_pos].T, (reps, 1))
    return wm.astype(BF16), jnp.repeat(bias, GROUP_W, axis=1).astype(F32)


def kernel(x_prompt, x_sample, cache_k, cache_v, w_in, lam_q1, lam_k1, lam_q2, lam_k2, subln_g,
           sgu_ln_g, sgu_ln_b, w_spatial, b_spatial, w_out, ln_mix_g, ln_mix_b,
           w_ffn_gate, w_ffn_up, w_ffn_down, w_router, w_moe_gate, w_moe_up, w_moe_down,
           ln_ffn_g, ln_ffn_b):
    depth = w_in.shape[0]
    batch, seq, d_model = x_prompt.shape
    dec_batch, dec_seq, _ = x_sample.shape
    past_len = cache_k.shape[2]
    alpha = (2.0 * depth) ** 0.25
    t_p, t_s = batch * seq, dec_batch * dec_seq
    n_keys = past_len + dec_seq
    keys_pad = -(-n_keys // LANES) * LANES

    tm_p = min(512, t_p)
    tm_s = min(512, t_s)
    tm_moe = math.gcd(tm_p, tm_s)
    tq = min(512, seq)

    xp = x_prompt.reshape(t_p, d_model)
    xs = x_sample.reshape(t_s, d_model)
    k_p, k_s, gv_s, v_p, v_s = [], [], [], [], []
    row2 = lambda a: a.reshape(1, -1).astype(F32)

    for l in range(depth):
        lam0 = _lambda_init(l)
        lam = (jnp.exp(jnp.sum(lam_q1[l].astype(F32) * lam_k1[l].astype(F32)))
               - jnp.exp(jnp.sum(lam_q2[l].astype(F32) * lam_k2[l].astype(F32))) + lam0).reshape(1)
        w_in_l = _cast_bf16(w_in, l)
        w_out_l = _cast_bf16(w_out, l)
        sub_g = row2(subln_g[l])
        sgu_g, sgu_b = row2(sgu_ln_g[l]), row2(sgu_ln_b[l])
        mix_g, mix_b = row2(ln_mix_g[l]), row2(ln_mix_b[l])
        ffn_g, ffn_b = row2(ln_ffn_g[l]), row2(ln_ffn_b[l])
        wm_p, bs_p = _spatial_params(w_spatial[l], b_spatial[l], CHUNK_MLP)
        wm_s, bs_s = _spatial_params(w_spatial[l], b_spatial[l], dec_seq)

        def channel_mix(x_p, x_s):
            i = l // 2
            if l % 2 == 0:
                wg, wu, wd = (_cast_bf16(w, i) for w in (w_ffn_gate, w_ffn_up, w_ffn_down))
                dense = lambda x, tm: _ffn_dense(x, wg, wu, wd, ffn_g, ffn_b, alpha, tm, 512)
                return dense(x_p, tm_p), dense(x_s, tm_s)
            wg, wu, wd = (_cast_bf16(w, i) for w in (w_moe_gate, w_moe_up, w_moe_down))
            return _moe(jnp.concatenate([x_p, x_s], axis=0), t_p, w_router[i].T.astype(F32), wg, wu, wd,
                        ffn_g, ffn_b, alpha, tm_moe, tm_moe, 512)

        q, k, v, kb, vb, u, gvn = _proj(xp, w_in_l, sgu_g, sgu_b, tm_p, BF16)
        k_p.append(k)
        v_p.append(v)
        shape3 = (batch, seq, -1)
        a = _attn_prompt(q.reshape(shape3), kb.reshape(shape3), vb.reshape(shape3), lam, sub_g, lam0, tq)
        xp = _merge(a.reshape(t_p, -1), u, gvn, xp, wm_p, bs_p, w_out_l, mix_g, mix_b, alpha, tm_p)

        q, k, v, kb, vb, u, gvn = _proj(xs, w_in_l, sgu_g, sgu_b, tm_s, F32)
        k_s.append(k)
        v_s.append(v)
        pad = lambda c, new: jnp.pad(
            jnp.concatenate([c.reshape(dec_batch, past_len, -1).astype(BF16),
                             new.reshape(dec_batch, dec_seq, -1)], axis=1),
            ((0, 0), (0, keys_pad - n_keys), (0, 0)))
        a = _attn_sample(q.reshape(dec_batch, dec_seq, -1), pad(cache_k[l], kb), pad(cache_v[l], vb),
                         n_keys, lam, sub_g, lam0)
        xs = _merge(a.reshape(t_s, -1), u, gvn, xs, wm_s, bs_s, w_out_l, mix_g, mix_b, alpha, tm_s)
        gv_s.append(gvn.reshape(dec_batch, dec_seq, -1))

        xp, xs = channel_mix(xp, xs)

    return (xp.reshape(batch, seq, d_model), xs.reshape(dec_batch, dec_seq, d_model),
            jnp.stack([k.reshape(batch, seq, N_HEADS, 2, D_HEAD) for k in k_p], axis=0),
            jnp.stack([v.reshape(batch, seq, N_HEADS, HEAD_W) for v in v_p], axis=0),
            jnp.stack([k.reshape(dec_batch, dec_seq, N_HEADS, 2, D_HEAD) for k in k_s], axis=0),
            jnp.stack([v.reshape(dec_batch, dec_seq, N_HEADS, HEAD_W) for v in v_s], axis=0),
            jnp.stack(gv_s, axis=0))
```

```python
import functools
import math

import jax
import jax.numpy as jnp
from jax import lax
from jax.experimental import pallas as pl
from jax.experimental.pallas import tpu as pltpu
from jax.experimental.pallas import tpu_sc as plsc

F32 = jnp.float32
BF16 = jnp.bfloat16

N_HEADS = 4
HEAD_W = 128
D_HEAD = 64
N_GROUPS = 4
GROUP_W = 128
CHUNK = 64
CHUNK_MLP = 128
N_EXPERTS = 8
GATE_ROWS = 8
LN_EPS = 1e-5
NEG_INF = -1e30

LANES = 128
MXU_DIM = 256
VMEM_LIMIT_BYTES = 56 * 1024 * 1024
CAST_ROWS = 256
UNROLL = 8


def _lambda_init(layer):
    return 0.8 - 0.6 * math.exp(-0.3 * layer)


def _layer_norm(x, g, b):
    mu = jnp.mean(x, axis=-1, keepdims=True)
    xc = x - mu
    var = jnp.mean(xc * xc, axis=-1, keepdims=True)
    return xc * lax.rsqrt(var + LN_EPS) * g + b


def _params(*semantics):
    return pltpu.CompilerParams(dimension_semantics=semantics, vmem_limit_bytes=VMEM_LIMIT_BYTES)


def _cast_kernel(w_ref, o_ref):
    o_ref[...] = w_ref[...].astype(o_ref.dtype)


def _cast_bf16(w, index):
    r, c = w.shape[-2:]
    w4 = w.reshape(w.shape[0], -1, r, c)
    rows = math.gcd(r, CAST_ROWS)
    out = pl.pallas_call(
        _cast_kernel,
        grid=(w4.shape[1], r // rows),
        in_specs=[pl.BlockSpec((None, None, rows, c), lambda e, j: (index, e, j, 0))],
        out_specs=pl.BlockSpec((None, rows, c), lambda e, j: (e, j, 0)),
        out_shape=jax.ShapeDtypeStruct(w4.shape[1:], BF16),
        compiler_params=_params("parallel", "parallel"),
        name="cast_bf16",
    )(w4)
    return out.reshape(w.shape[1:])


def _proj_kernel(x_ref, w_ref, g_ref, b_ref, q_ref, k_ref, v_ref, kb_ref, vb_ref, u_ref, gv_ref,
                 *, qk_w, d_a, d_b):
    x = x_ref[...].astype(BF16)

    def cols(start, width):
        return jnp.dot(x, w_ref[:, start:start + width], preferred_element_type=F32)

    q_ref[...] = (cols(0, qk_w) * (D_HEAD ** -0.5)).astype(BF16)
    k = cols(qk_w, qk_w)
    k_ref[...] = k
    kb_ref[...] = k.astype(BF16)
    v = cols(2 * qk_w, d_a)
    v_ref[...] = v
    vb_ref[...] = v.astype(BF16)
    off = 2 * qk_w + d_a
    u_ref[...] = jax.nn.gelu(cols(off, d_b)).astype(u_ref.dtype)
    gv = jax.nn.gelu(cols(off + d_b, d_b))
    gv_ref[...] = _layer_norm(gv, g_ref[...], b_ref[...]).astype(gv_ref.dtype)


def _proj(x, w_in, ln_g, ln_b, tm, gv_dtype):
    t, d = x.shape
    qk_w = N_HEADS * HEAD_W
    d_a = N_HEADS * HEAD_W
    d_b = N_GROUPS * GROUP_W
    assert w_in.shape == (d, 2 * qk_w + d_a + 2 * d_b) and t % tm == 0
    row = lambda w: pl.BlockSpec((tm, w), lambda i: (i, 0))
    const = lambda shape: pl.BlockSpec(shape, lambda i: (0, 0))
    return pl.pallas_call(
        functools.partial(_proj_kernel, qk_w=qk_w, d_a=d_a, d_b=d_b),
        grid=(t // tm,),
        in_specs=[row(d), const(w_in.shape), const((1, d_b)), const((1, d_b))],
        out_specs=[row(qk_w), row(qk_w), row(d_a), row(qk_w), row(d_a), row(d_b), row(d_b)],
        out_shape=[
            jax.ShapeDtypeStruct((t, qk_w), BF16),
            jax.ShapeDtypeStruct((t, qk_w), F32),
            jax.ShapeDtypeStruct((t, d_a), F32),
            jax.ShapeDtypeStruct((t, qk_w), BF16),
            jax.ShapeDtypeStruct((t, d_a), BF16),
            jax.ShapeDtypeStruct((t, d_b), BF16),
            jax.ShapeDtypeStruct((t, d_b), gv_dtype),
        ],
        compiler_params=_params("parallel"),
        name="proj",
    )(x, w_in, ln_g, ln_b)


def _stack_queries(q):
    lane = lax.broadcasted_iota(jnp.int32, q.shape, 1)
    zero = jnp.zeros_like(q)
    return jnp.concatenate([jnp.where(lane < D_HEAD, q, zero), jnp.where(lane >= D_HEAD, q, zero)], axis=0)


def _scores(qs, kb):
    return lax.dot_general(qs, kb, (((1,), (1,)), ((), ())), preferred_element_type=F32)


def _diff_merge(o1, o2, lam, g, lam0):
    o = o1 - lam * o2
    ms = jnp.mean(o * o, axis=-1, keepdims=True)
    return o * lax.rsqrt(ms + LN_EPS) * g * (1.0 - lam0)


def _attn_prompt_kernel(lam_ref, g_ref, q_ref, k_ref, v_ref, o_ref,
                        qs_ref, s_ref, p_ref, a_ref, m_ref, acc_ref, *, tq, lam0):
    i = pl.program_id(2)
    n_rep = tq // LANES
    diag_slot = UNROLL - 1
    qs_ref[...] = _stack_queries(q_ref[...])
    m_ref[...] = jnp.full(m_ref.shape, NEG_INF, F32)
    acc_ref[...] = jnp.zeros(acc_ref.shape, F32)
    ones = jnp.ones((tq, HEAD_W), BF16)

    def scores(j, slot):
        off = pl.multiple_of(j * tq, tq)
        s_ref[slot] = _scores(qs_ref[...], k_ref[pl.ds(off, tq), :])

    def softmax(slot, masked):
        for r0 in range(0, 2 * tq, CHUNK):
            rows = slice(r0, r0 + CHUNK)
            s = s_ref[slot, rows, :]
            if masked:
                n_vis = (r0 % tq) + CHUNK
                s = jnp.where(lax.broadcasted_iota(jnp.int32, s.shape, 1) < n_vis, s, NEG_INF)
            m_prev = m_ref[rows, :]
            m_new = jnp.maximum(m_prev, jnp.max(s, axis=1, keepdims=True))
            a_ref[slot, rows, :] = jnp.exp(m_prev - m_new)
            p_ref[slot, rows, :] = jnp.exp(s - jnp.concatenate([m_new] * n_rep, axis=1)).astype(BF16)
            m_ref[rows, :] = m_new

    def pv(j, slot):
        off = pl.multiple_of(j * tq, tq)
        v_ext = jnp.concatenate([v_ref[pl.ds(off, tq), :], ones], axis=1)
        a = a_ref[slot]
        acc_ref[...] = (jnp.concatenate([a, a], axis=1) * acc_ref[...]
                        + jnp.dot(p_ref[slot], v_ext, preferred_element_type=F32))

    def step(t, slot, next_scores=True):
        if next_scores:
            scores(t + 1, (slot + 1) % UNROLL)
        pv(jnp.where(t == 0, i, t - 1), (slot - 1) % UNROLL)
        softmax(slot, False)

    scores(i, diag_slot)
    scores(0, 0)
    softmax(diag_slot, True)

    def group(w, carry):
        for k in range(UNROLL):
            step(UNROLL * w + k, k)
        return carry

    lax.fori_loop(0, i // UNROLL, group, 0)

    base = (i // UNROLL) * UNROLL
    for rem in range(UNROLL):
        @pl.when(i % UNROLL == rem)
        def _(rem=rem):
            for k in range(rem):
                step(base + k, k, next_scores=k < rem - 1)
            pv(jnp.maximum(i - 1, 0), (rem - 1) % UNROLL)

    acc = acc_ref[...]
    o1 = acc[:tq, :HEAD_W] / acc[:tq, HEAD_W:]
    o2 = acc[tq:, :HEAD_W] / acc[tq:, HEAD_W:]
    o_ref[...] = _diff_merge(o1, o2, lam_ref[0], g_ref[...], lam0).astype(o_ref.dtype)


def _attn_prompt(q, k, v, lam, subln_g, lam0, tq):
    b, s, w = q.shape
    assert s % tq == 0 and tq % CHUNK == 0 and tq % LANES == 0 and w == N_HEADS * HEAD_W
    q_spec = pl.BlockSpec((None, tq, HEAD_W), lambda bi, h, i: (bi, i, h))
    kv_spec = pl.BlockSpec((None, s, HEAD_W), lambda bi, h, i: (bi, 0, h))
    return pl.pallas_call(
        functools.partial(_attn_prompt_kernel, tq=tq, lam0=lam0),
        grid=(b, N_HEADS, s // tq),
        in_specs=[
            pl.BlockSpec(memory_space=pltpu.SMEM),
            pl.BlockSpec((1, HEAD_W), lambda bi, h, i: (0, 0)),
            q_spec, kv_spec, kv_spec,
        ],
        out_specs=q_spec,
        out_shape=jax.ShapeDtypeStruct((b, s, w), BF16),
        scratch_shapes=[
            pltpu.VMEM((2 * tq, HEAD_W), BF16),
            pltpu.VMEM((UNROLL, 2 * tq, tq), F32),
            pltpu.VMEM((UNROLL, 2 * tq, tq), BF16),
            pltpu.VMEM((UNROLL, 2 * tq, LANES), F32),
            pltpu.VMEM((2 * tq, LANES), F32),
            pltpu.VMEM((2 * tq, 2 * HEAD_W), F32),
        ],
        compiler_params=_params("parallel", "parallel", "arbitrary"),
        name="attn_prompt",
    )(lam, subln_g, q, k, v)


def _attn_sample_kernel(lam_ref, g_ref, q_ref, k_ref, v_ref, o_ref, *, n_q, n_keys, lam0):
    qs = _stack_queries(q_ref[...])
    s = _scores(qs, k_ref[...])
    col = lax.broadcasted_iota(jnp.int32, s.shape, 1)
    s = jnp.where(col < n_keys, s, NEG_INF)
    m = jnp.max(s, axis=1, keepdims=True)
    p = jnp.exp(s - m)
    l = jnp.sum(p, axis=1, keepdims=True)
    o = jnp.dot(p.astype(BF16), v_ref[...], preferred_element_type=F32) / l
    o_ref[...] = _diff_merge(o[:n_q], o[n_q:], lam_ref[0], g_ref[...], lam0).astype(o_ref.dtype)


def _attn_sample(q, k_all, v_all, n_keys, lam, subln_g, lam0):
    b, n_q, w = q.shape
    n_pad = k_all.shape[1]
    q_spec = pl.BlockSpec((None, n_q, HEAD_W), lambda bi, h: (bi, 0, h))
    kv_spec = pl.BlockSpec((None, n_pad, HEAD_W), lambda bi, h: (bi, 0, h))
    return pl.pallas_call(
        functools.partial(_attn_sample_kernel, n_q=n_q, n_keys=n_keys, lam0=lam0),
        grid=(b, N_HEADS),
        in_specs=[
            pl.BlockSpec(memory_space=pltpu.SMEM),
            pl.BlockSpec((1, HEAD_W), lambda bi, h: (0, 0)),
            q_spec, kv_spec, kv_spec,
        ],
        out_specs=q_spec,
        out_shape=jax.ShapeDtypeStruct((b, n_q, w), BF16),
        compiler_params=_params("parallel", "parallel"),
        name="attn_sample",
    )(lam, subln_g, q, k_all, v_all)


def _merge_kernel(a_ref, u_ref, gv_ref, x_ref, wm_ref, bs_ref, wo_ref, g_ref, b_ref, y_ref, gated_ref,
                  *, tm, alpha):
    d_a = a_ref.shape[1]
    for c in range(tm // CHUNK_MLP):
        rows = slice(c * CHUNK_MLP, (c + 1) * CHUNK_MLP)
        for grp in range(N_GROUPS):
            cs = slice(grp * GROUP_W, (grp + 1) * GROUP_W)
            mixed = jnp.dot(wm_ref[grp], gv_ref[rows, cs].astype(BF16), preferred_element_type=F32)
            gated_ref[rows, cs] = (u_ref[rows, cs] * (mixed + bs_ref[:, cs])).astype(BF16)
    mix = jnp.dot(a_ref[...], wo_ref[:d_a, :], preferred_element_type=F32)
    mix = mix + jnp.dot(gated_ref[...], wo_ref[d_a:, :], preferred_element_type=F32)
    y_ref[...] = _layer_norm(alpha * x_ref[...] + mix, g_ref[...], b_ref[...])


def _merge(a, u, gvn, x, wm, bs, w_out, ln_g, ln_b, alpha, tm):
    t, d = x.shape
    d_a, d_b = a.shape[1], u.shape[1]
    assert t % tm == 0 and tm % CHUNK_MLP == 0
    row = lambda w: pl.BlockSpec((tm, w), lambda i: (i, 0))
    const = lambda shape: pl.BlockSpec(shape, lambda i: (0,) * len(shape))
    return pl.pallas_call(
        functools.partial(_merge_kernel, tm=tm, alpha=alpha),
        grid=(t // tm,),
        in_specs=[row(d_a), row(d_b), row(d_b), row(d), const(wm.shape), const(bs.shape),
                  const(w_out.shape), const((1, d)), const((1, d))],
        out_specs=row(d),
        out_shape=jax.ShapeDtypeStruct((t, d), F32),
        scratch_shapes=[pltpu.VMEM((tm, d_b), BF16)],
        compiler_params=_params("parallel"),
        name="merge",
    )(a, u, gvn, x, wm, bs, w_out, ln_g, ln_b)


def _ff_chunks(d_ff, width):
    return [(s, min(width, d_ff - s)) for s in range(0, d_ff, width)]


def _swiglu_tile(x, wg_ref, wu_ref, wd_ref, ff_chunk):
    acc = None
    for start, size in _ff_chunks(wg_ref.shape[1], ff_chunk):
        hg = jnp.dot(x, wg_ref[:, start:start + size], preferred_element_type=F32)
        hu = jnp.dot(x, wu_ref[:, start:start + size], preferred_element_type=F32)
        h = (jax.nn.silu(hg) * hu).astype(BF16)
        part = jnp.dot(h, wd_ref[start:start + size, :], preferred_element_type=F32)
        acc = part if acc is None else acc + part
    return acc


def _ffn_dense_kernel(x_ref, wg_ref, wu_ref, wd_ref, g_ref, b_ref, y_ref, *, alpha, ff_chunk):
    x = x_ref[...]
    ffn = _swiglu_tile(x.astype(BF16), wg_ref, wu_ref, wd_ref, ff_chunk)
    y_ref[...] = _layer_norm(alpha * x + ffn, g_ref[...], b_ref[...])


def _ffn_dense(x, wg, wu, wd, ln_g, ln_b, alpha, tm, ff_chunk):
    t, d = x.shape
    assert t % tm == 0
    row = pl.BlockSpec((tm, d), lambda i: (i, 0))
    const = lambda shape: pl.BlockSpec(shape, lambda i: (0, 0), pipeline_mode=pl.Buffered(1))
    return pl.pallas_call(
        functools.partial(_ffn_dense_kernel, alpha=alpha, ff_chunk=ff_chunk),
        grid=(t // tm,),
        in_specs=[row, const(wg.shape), const(wu.shape), const(wd.shape), const((1, d)), const((1, d))],
        out_specs=row,
        out_shape=jax.ShapeDtypeStruct((t, d), F32),
        compiler_params=_params("parallel"),
        name="ffn_dense",
    )(x, wg, wu, wd, ln_g, ln_b)


def _ffn_grouped_kernel(te_ref, nt_ref, x_ref, wg_ref, wu_ref, wd_ref, y_ref, *, ff_chunk):
    @pl.when(pl.program_id(0) < nt_ref[0])
    def _():
        y_ref[...] = _swiglu_tile(x_ref[...].astype(BF16), wg_ref, wu_ref, wd_ref, ff_chunk)


def _ffn_grouped(xs, tile_expert, n_tiles_used, wg, wu, wd, tm, ff_chunk):
    r, d = xs.shape
    assert r % tm == 0
    row = pl.BlockSpec((tm, d), lambda i, te, nt: (i, 0))
    expert = lambda shape: pl.BlockSpec((None,) + shape[1:], lambda i, te, nt: (te[i], 0, 0),
                                        pipeline_mode=pl.Buffered(1))
    return pl.pallas_call(
        functools.partial(_ffn_grouped_kernel, ff_chunk=ff_chunk),
        grid_spec=pltpu.PrefetchScalarGridSpec(
            num_scalar_prefetch=2,
            grid=(r // tm,),
            in_specs=[row, expert(wg.shape), expert(wu.shape), expert(wd.shape)],
            out_specs=row,
        ),
        out_shape=jax.ShapeDtypeStruct((r, d), F32),
        compiler_params=_params("arbitrary"),
        name="ffn_grouped",
    )(tile_expert, n_tiles_used, xs, wg, wu, wd)


def _router_kernel(x_ref, wr_ref, e_ref, g_ref, r_ref, c_ref):
    logits = lax.dot_general(wr_ref[...], x_ref[...], (((1,), (1,)), ((), ())),
                             precision=lax.Precision.HIGHEST, preferred_element_type=F32)
    tm = logits.shape[1]
    idx = lax.broadcasted_iota(jnp.int32, logits.shape, 0)
    m1 = jnp.max(logits, axis=0, keepdims=True)
    i1 = jnp.min(jnp.where(logits == m1, idx, N_EXPERTS), axis=0, keepdims=True)
    rest = jnp.where(idx == i1, -jnp.inf, logits)
    m2 = jnp.max(rest, axis=0, keepdims=True)
    i2 = jnp.min(jnp.where(rest == m2, idx, N_EXPERTS), axis=0, keepdims=True)
    e2 = jnp.exp(m2 - m1)
    denom = 1.0 + e2
    e_ref[...] = jnp.concatenate([i1, i2], axis=0)
    g_ref[...] = jnp.concatenate([1.0 / denom, e2 / denom, jnp.zeros((GATE_ROWS - 2, tm), F32)], axis=0)
    hit1, hit2 = idx == i1, idx == i2
    hit = jnp.where(jnp.logical_or(hit1, hit2), 1.0, 0.0)
    earlier = jnp.where(lax.broadcasted_iota(jnp.int32, (tm, tm), 0) < lax.broadcasted_iota(jnp.int32, (tm, tm), 1),
                        1.0, 0.0).astype(BF16)
    before = jnp.dot(hit.astype(BF16), earlier, preferred_element_type=F32)
    r_ref[...] = jnp.concatenate([jnp.sum(jnp.where(hit1, before, 0.0), axis=0, keepdims=True),
                                  jnp.sum(jnp.where(hit2, before, 0.0), axis=0, keepdims=True)],
                                 axis=0).astype(jnp.int32)
    c_ref[...] = jnp.broadcast_to(jnp.sum(hit, axis=1, keepdims=True), c_ref.shape)


def _router(x, w_router_t, tm):
    t, d = x.shape
    assert t % tm == 0
    pair = pl.BlockSpec((2, tm), lambda i: (0, i))
    experts, gates, ranks, counts = pl.pallas_call(
        _router_kernel,
        grid=(t // tm,),
        in_specs=[pl.BlockSpec((tm, d), lambda i: (i, 0)), pl.BlockSpec(w_router_t.shape, lambda i: (0, 0))],
        out_specs=[pair, pl.BlockSpec((GATE_ROWS, tm), lambda i: (0, i)), pair,
                   pl.BlockSpec((None, N_EXPERTS, LANES), lambda i: (i, 0, 0))],
        out_shape=[jax.ShapeDtypeStruct((2, t), jnp.int32), jax.ShapeDtypeStruct((GATE_ROWS, t), F32),
                   jax.ShapeDtypeStruct((2, t), jnp.int32),
                   jax.ShapeDtypeStruct((t // tm, N_EXPERTS, LANES), F32)],
        compiler_params=_params("parallel"),
        name="router",
    )(x, w_router_t)
    return experts, gates, ranks, counts[:, :, 0].astype(jnp.int32)


SC_WINDOW = 128
SC_COLS = 256


def _sc_mesh():
    return plsc.VectorSubcoreMesh(core_axis_name="core", subcore_axis_name="subcore")


def _sc_pipeline(body, n_rows, in_specs, out_specs):
    return pltpu.emit_pipeline(body, grid=(n_rows // SC_WINDOW,), in_specs=in_specs, out_specs=out_specs,
                               core_axis_name=("core", "subcore"), dimension_semantics=(pltpu.PARALLEL,))


def _sc_dispatch(x, pos, n_rows):
    t, d = x.shape
    assert t % SC_WINDOW == 0 and d % SC_COLS == 0

    @pl.kernel(out_type=jax.ShapeDtypeStruct((n_rows, d), x.dtype), mesh=_sc_mesh())
    def dispatch(x_hbm, pos_hbm, o_hbm):
        for s in range(pos.shape[0]):
            for c in range(d // SC_COLS):
                def body(x_vmem, pos_vmem, c=c):
                    pltpu.sync_copy(x_vmem, o_hbm.at[pos_vmem.at[0], pl.ds(c * SC_COLS, SC_COLS)])

                _sc_pipeline(body, t,
                             [pl.BlockSpec((SC_WINDOW, SC_COLS), lambda i, c=c: (i, c)),
                              pl.BlockSpec((1, SC_WINDOW), lambda i, s=s: (s, i))],
                             [])(x_hbm, pos_hbm)

    return dispatch(x, pos)


def _sc_gather(y, pos):
    n = pos.shape[0]
    d = y.shape[1]
    assert n % SC_WINDOW == 0 and d % SC_COLS == 0

    @pl.kernel(out_type=jax.ShapeDtypeStruct((n, d), y.dtype), mesh=_sc_mesh())
    def gather(y_hbm, pos_hbm, o_hbm):
        for c in range(d // SC_COLS):
            def body(pos_vmem, o_vmem, c=c):
                pltpu.sync_copy(y_hbm.at[pos_vmem.at[0], pl.ds(c * SC_COLS, SC_COLS)], o_vmem)

            _sc_pipeline(body, n,
                         [pl.BlockSpec((1, SC_WINDOW), lambda i: (0, i))],
                         [pl.BlockSpec((SC_WINDOW, SC_COLS), lambda i, c=c: (i, c))])(pos_hbm, o_hbm)

    return gather(y, pos.reshape(1, n))


def _combine_kernel(x_ref, y1_ref, y2_ref, g_ref, lg_ref, lb_ref, oa_ref, ob_ref, *, alpha, n_a):
    gates = g_ref[...].T
    ffn = gates[:, 0:1] * y1_ref[...] + gates[:, 1:2] * y2_ref[...]
    out = _layer_norm(alpha * x_ref[...] + ffn, lg_ref[...], lb_ref[...])

    @pl.when(pl.program_id(0) < n_a)
    def _():
        oa_ref[...] = out

    @pl.when(pl.program_id(0) >= n_a)
    def _():
        ob_ref[...] = out


def _combine(x, y_pairs, gates, ln_g, ln_b, alpha, tm, t_a):
    t, d = x.shape
    n_blk, n_a = t // tm, t_a // tm
    assert t % tm == 0 and t_a % tm == 0 and 0 < n_a < n_blk
    row = pl.BlockSpec((tm, d), lambda i: (i, 0))
    row2 = pl.BlockSpec((tm, d), lambda i: (i + n_blk, 0))
    gate = pl.BlockSpec((GATE_ROWS, tm), lambda i: (0, i))
    const = pl.BlockSpec((1, d), lambda i: (0, 0))
    return pl.pallas_call(
        functools.partial(_combine_kernel, alpha=alpha, n_a=n_a),
        grid=(n_blk,),
        in_specs=[row, row, row2, gate, const, const],
        out_specs=[pl.BlockSpec((tm, d), lambda i: (jnp.minimum(i, n_a - 1), 0)),
                   pl.BlockSpec((tm, d), lambda i: (jnp.maximum(i - n_a, 0), 0))],
        out_shape=[jax.ShapeDtypeStruct((t_a, d), F32), jax.ShapeDtypeStruct((t - t_a, d), F32)],
        compiler_params=_params("arbitrary"),
        name="combine",
    )(x, y_pairs, y_pairs, gates, ln_g, ln_b)


def _moe(x, t_a, w_router_t, wg, wu, wd, ln_g, ln_b, alpha, tm_route, tm_ffn, ff_chunk):
    t, d = x.shape
    n_blk = t // tm_route
    experts, gates, ranks, counts = _router(x, w_router_t, tm_route)
    total = jnp.sum(counts, axis=0)
    tiles_per = (total + tm_ffn - 1) // tm_ffn
    tile_end = jnp.cumsum(tiles_per)
    starts = (tile_end - tiles_per) * tm_ffn
    base = starts[None, :] + jnp.cumsum(counts, axis=0) - counts
    experts3 = experts.reshape(2, n_blk, tm_route)
    pos = ranks.reshape(2, n_blk, tm_route)
    for e in range(N_EXPERTS):
        pos = pos + jnp.where(experts3 == e, base[None, :, e, None], 0)
    pos = pos.reshape(2, t)
    n_tiles = (2 * t) // tm_ffn + N_EXPERTS
    tile_ids = jnp.arange(n_tiles, dtype=jnp.int32)
    tile_expert = jnp.minimum(jnp.sum((tile_end[None, :] <= tile_ids[:, None]).astype(jnp.int32), axis=1),
                              N_EXPERTS - 1)
    xs = _sc_dispatch(x, pos, n_tiles * tm_ffn)
    ys = _ffn_grouped(xs, tile_expert, tile_end[-1:].astype(jnp.int32), wg, wu, wd, tm_ffn, ff_chunk)
    y_pairs = _sc_gather(ys, pos.reshape(2 * t))
    return _combine(x, y_pairs, gates, ln_g, ln_b, alpha, tm_route, t_a)


def _spatial_params(w_s, b_s, n_pos):
    reps = CHUNK_MLP // n_pos
    wm = w_s[:, :n_pos, :n_pos] * jnp.tril(jnp.ones((n_pos, n_pos), w_s.dtype))
    eye = jnp.eye(reps, dtype=w_s.dtype)
    wm = jnp.einsum("ab,gij->gaibj", eye, wm).reshape(N_GROUPS, CHUNK_MLP, CHUNK_MLP)
    bias = jnp.tile(b_s[:, :n_pos].T, (reps, 1))
    return wm.astype(BF16), jnp.repeat(bias, GROUP_W, axis=1).astype(F32)


def kernel(x_prompt, x_sample, cache_k, cache_v, w_in, lam_q1, lam_k1, lam_q2, lam_k2, subln_g,
           sgu_ln_g, sgu_ln_b, w_spatial, b_spatial, w_out, ln_mix_g, ln_mix_b,
           w_ffn_gate, w_ffn_up, w_ffn_down, w_router, w_moe_gate, w_moe_up, w_moe_down,
           ln_ffn_g, ln_ffn_b):
    depth = w_in.shape[0]
    batch, seq, d_model = x_prompt.shape
    dec_batch, dec_seq, _ = x_sample.shape
    past_len = cache_k.shape[2]
    alpha = (2.0 * depth) ** 0.25
    t_p, t_s = batch * seq, dec_batch * dec_seq
    n_keys = past_len + dec_seq
    keys_pad = -(-n_keys // LANES) * LANES

    tm_p = min(512, t_p)
    tm_proj = math.gcd(2 * tm_p, t_p)
    tm_s = min(512, t_s)
    tm_moe = math.gcd(tm_p, tm_s)
    tq = min(512, seq)

    xp = x_prompt.reshape(t_p, d_model)
    xs = x_sample.reshape(t_s, d_model)
    k_p, k_s, gv_s, v_p, v_s = [], [], [], [], []
    row2 = lambda a: a.reshape(1, -1).astype(F32)

    for l in range(depth):
        lam0 = _lambda_init(l)
        lam = (jnp.exp(jnp.sum(lam_q1[l].astype(F32) * lam_k1[l].astype(F32)))
               - jnp.exp(jnp.sum(lam_q2[l].astype(F32) * lam_k2[l].astype(F32))) + lam0).reshape(1)
        w_in_l = _cast_bf16(w_in, l)
        w_out_l = _cast_bf16(w_out, l)
        sub_g = row2(subln_g[l])
        sgu_g, sgu_b = row2(sgu_ln_g[l]), row2(sgu_ln_b[l])
        mix_g, mix_b = row2(ln_mix_g[l]), row2(ln_mix_b[l])
        ffn_g, ffn_b = row2(ln_ffn_g[l]), row2(ln_ffn_b[l])
        wm_p, bs_p = _spatial_params(w_spatial[l], b_spatial[l], CHUNK_MLP)
        wm_s, bs_s = _spatial_params(w_spatial[l], b_spatial[l], dec_seq)

        def channel_mix(x_p, x_s):
            i = l // 2
            if l % 2 == 0:
                wg, wu, wd = (_cast_bf16(w, i) for w in (w_ffn_gate, w_ffn_up, w_ffn_down))
                dense = lambda x, tm: _ffn_dense(x, wg, wu, wd, ffn_g, ffn_b, alpha, tm, 512)
                return dense(x_p, tm_p), dense(x_s, tm_s)
            wg, wu, wd = (_cast_bf16(w, i) for w in (w_moe_gate, w_moe_up, w_moe_down))
            return _moe(jnp.concatenate([x_p, x_s], axis=0), t_p, w_router[i].T.astype(F32), wg, wu, wd,
                        ffn_g, ffn_b, alpha, tm_moe, tm_moe, 512)

        q, k, v, kb, vb, u, gvn = _proj(xp, w_in_l, sgu_g, sgu_b, tm_proj, BF16)
        k_p.append(k)
        v_p.append(v)
        shape3 = (batch, seq, -1)
        a = _attn_prompt(q.reshape(shape3), kb.reshape(shape3), vb.reshape(shape3), lam, sub_g, lam0, tq)
        xp = _merge(a.reshape(t_p, -1), u, gvn, xp, wm_p, bs_p, w_out_l, mix_g, mix_b, alpha, tm_p)

        q, k, v, kb, vb, u, gvn = _proj(xs, w_in_l, sgu_g, sgu_b, tm_s, F32)
        k_s.append(k)
        v_s.append(v)
        pad = lambda c, new: jnp.pad(
            jnp.concatenate([c.reshape(dec_batch, past_len, -1).astype(BF16),
                             new.reshape(dec_batch, dec_seq, -1)], axis=1),
            ((0, 0), (0, keys_pad - n_keys), (0, 0)))
        a = _attn_sample(q.reshape(dec_batch, dec_seq, -1), pad(cache_k[l], kb), pad(cache_v[l], vb),
                         n_keys, lam, sub_g, lam0)
        xs = _merge(a.reshape(t_s, -1), u, gvn, xs, wm_s, bs_s, w_out_l, mix_g, mix_b, alpha, tm_s)
        gv_s.append(gvn.reshape(dec_batch, dec_seq, -1))

        xp, xs = channel_mix(xp, xs)

    return (xp.reshape(batch, seq, d_model), xs.reshape(dec_batch, dec_seq, d_model),
            jnp.stack([k.reshape(batch, seq, N_HEADS, 2, D_HEAD) for k in k_p], axis=0),
            jnp.stack([v.reshape(batch, seq, N_HEADS, HEAD_W) for v in v_p], axis=0),
            jnp.stack([k.reshape(dec_batch, dec_seq, N_HEADS, 2, D_HEAD) for k in k_s], axis=0),
            jnp.stack([v.reshape(dec_batch, dec_seq, N_HEADS, HEAD_W) for v in v_s], axis=0),
            jnp.stack(gv_s, axis=0))
```
